```python
import math
import jax
import jax.numpy as jnp
from jax import lax

D_MODEL = 1024
BATCH = 1
SEQ = 16384
DEPTH = 4

CHUNK = 64
NORM_EPS = 1e-6
CONV_K = 4
D_FF = 2816
N_MOD = 9

GDN_HEADS = 4
GDN_DK = 128
GDN_DV = 128
GDN_WIDTH = GDN_HEADS * GDN_DV
GDN_CONV_CH = 2 * GDN_HEADS * GDN_DK + GDN_WIDTH

MLA_HEADS = 4
MLA_Q_RANK = 384
MLA_KV_RANK = 256
MLA_NOPE = 128
MLA_ROPE = 64
MLA_DV = 128
MLA_WIDTH = MLA_HEADS * MLA_DV
ROPE_THETA = 10000.0
Q_BLOCK = 128

MIX_WIDTH = GDN_WIDTH + MLA_WIDTH
EVEN_SPLITS = (GDN_CONV_CH, GDN_WIDTH, GDN_HEADS, GDN_HEADS, MLA_Q_RANK, MLA_KV_RANK, MLA_ROPE)
EVEN_IN = GDN_CONV_CH + GDN_WIDTH + 2 * GDN_HEADS + MLA_Q_RANK + MLA_KV_RANK + MLA_ROPE

SSD_D_INNER = 2 * D_MODEL
SSD_HEADDIM = 64
SSD_HEADS = SSD_D_INNER // SSD_HEADDIM
SSD_GROUPS = 4
SSD_STATE = 128
SSD_CONV_CH = SSD_D_INNER + 2 * SSD_GROUPS * SSD_STATE
ODD_SPLITS = (SSD_D_INNER, SSD_CONV_CH, SSD_HEADS)
ODD_IN = SSD_D_INNER + SSD_CONV_CH + SSD_HEADS

N_EVEN = (DEPTH + 1) // 2
N_ODD = DEPTH // 2

kernel_name = "hybrid_gdn_mla_mamba2_macaron_adaln"


def split_cols(t, sizes):
    out, start = [], 0
    for n in sizes:
        out.append(t[..., start:start + n])
        start += n
    return out


def rmsnorm(x, g):
    xf = x.astype(jnp.float32)
    y = xf * lax.rsqrt(jnp.mean(xf * xf, axis=-1, keepdims=True) + NORM_EPS)
    return (y * g.astype(jnp.float32)).astype(x.dtype)


def l2norm(x):
    xf = x.astype(jnp.float32)
    return xf * lax.rsqrt(jnp.sum(xf * xf, axis=-1, keepdims=True) + NORM_EPS)


def adaln_norm(x, g, shift, scale):
    return rmsnorm(x, g) * (1.0 + scale[:, None, :]) + shift[:, None, :]


def swiglu(h, w1, w3, w2):
    return (jax.nn.silu(h @ w1) * (h @ w3)) @ w2


def causal_conv(x, w, b=None):
    k_w = w.shape[0]
    s = x.shape[1]
    xp = jnp.pad(x, ((0, 0), (k_w - 1, 0), (0, 0)))
    y = sum(w[j] * xp[:, j:j + s] for j in range(k_w))
    return y if b is None else y + b


def rope_tables(positions, dim):
    half = dim // 2
    inv_freq = ROPE_THETA ** (-jnp.arange(half, dtype=jnp.float32) / half)
    ang = positions.astype(jnp.float32)[..., None] * inv_freq
    return jnp.cos(ang), jnp.sin(ang)


def apply_rope(x, cos, sin):
    half = x.shape[-1] // 2
    x1, x2 = x[..., :half], x[..., half:]
    return jnp.concatenate([x1 * cos - x2 * sin, x2 * cos + x1 * sin], axis=-1).astype(x.dtype)


def gated_delta_rule(q, k, v, beta_logit, a, a_log, dt_bias):
    b_, s, h, _ = q.shape
    nc = s // CHUNK
    f32 = jnp.float32
    q = l2norm(q) * (GDN_DK ** -0.5)
    k = l2norm(k)
    v = v.astype(f32)
    beta = jax.nn.sigmoid(beta_logit.astype(f32))
    g = -jnp.exp(a_log.astype(f32)) * jax.nn.softplus(a.astype(f32) + dt_bias.astype(f32))

    def to_chunks(t):
        return jnp.swapaxes(t.reshape(b_, nc, CHUNK, h, *t.shape[3:]), 2, 3)

    qc, kc, vc, bc = to_chunks(q), to_chunks(k), to_chunks(v), to_chunks(beta)
    gc = jnp.cumsum(to_chunks(g), axis=-1)
    lower = jnp.tril(jnp.ones((CHUNK, CHUNK), dtype=bool))
    strict = jnp.tril(jnp.ones((CHUNK, CHUNK), dtype=bool), -1)
    decay = jnp.exp(jnp.where(lower, gc[..., :, None] - gc[..., None, :], -jnp.inf))
    kb = kc * bc[..., None]
    a_strict = jnp.where(strict, jnp.einsum("bnhid,bnhjd->bnhij", kb, kc) * decay, 0.0)
    t_mat = a_strict + jnp.eye(CHUNK, dtype=f32)
    rhs = jnp.concatenate([vc * bc[..., None], kb * jnp.exp(gc)[..., None]], axis=-1)
    sol = lax.linalg.triangular_solve(t_mat, rhs, left_side=True, lower=True, unit_diagonal=True)
    u, w = sol[..., :GDN_DV], sol[..., GDN_DV:]
    attn = jnp.einsum("bnhid,bnhjd->bnhij", qc, kc) * decay
    g_last = gc[..., -1]
    k_end = kc * jnp.exp(g_last[..., None] - gc)[..., None]
    q_start = qc * jnp.exp(gc)[..., None]

    def step(state, inp):
        u_i, w_i, q_i, k_i, a_i, gl_i = inp
        v_new = u_i - jnp.einsum("bhck,bhkv->bhcv", w_i, state)
        o_i = jnp.einsum("bhck,bhkv->bhcv", q_i, state) + jnp.einsum("bhij,bhjv->bhiv", a_i, v_new)
        state = state * jnp.exp(gl_i)[..., None, None] + jnp.einsum("bhck,bhcv->bhkv", k_i, v_new)
        return state, o_i

    s0 = jnp.zeros((b_, h, GDN_DK, GDN_DV), f32)
    xs = (jnp.moveaxis(u, 1, 0), jnp.moveaxis(w, 1, 0), jnp.moveaxis(q_start, 1, 0),
          jnp.moveaxis(k_end, 1, 0), jnp.moveaxis(attn, 1, 0), jnp.moveaxis(g_last, 1, 0))
    _, o = lax.scan(step, s0, xs)
    return jnp.transpose(o, (1, 0, 3, 2, 4)).reshape(b_, s, h, GDN_DV)


def latent_attention(c_q, c_kv, k_rope, positions, q_norm_g, w_uq, kv_norm_g, w_ukv):
    b_, s, _ = c_q.shape
    q = jnp.einsum("bsr,rhd->bshd", rmsnorm(c_q, q_norm_g), w_uq)
    q_nope, q_pe = q[..., :MLA_NOPE], q[..., MLA_NOPE:]
    kv = jnp.einsum("bsr,rhd->bshd", rmsnorm(c_kv, kv_norm_g), w_ukv)
    k_nope, v = kv[..., :MLA_NOPE], kv[..., MLA_NOPE:]
    cos, sin = rope_tables(positions, MLA_ROPE)
    q_pe = apply_rope(q_pe, cos[:, :, None, :], sin[:, :, None, :])
    k_pe = apply_rope(k_rope, cos, sin)
    scale = (MLA_NOPE + MLA_ROPE) ** -0.5
    key_chunk = jnp.arange(s) // CHUNK

    def query_block(i):
        start = i * Q_BLOCK
        qn = lax.dynamic_slice_in_dim(q_nope, start, Q_BLOCK, axis=1)
        qp = lax.dynamic_slice_in_dim(q_pe, start, Q_BLOCK, axis=1)
        sc = jnp.einsum("bqhd,bkhd->bhqk", qn, k_nope) + jnp.einsum("bqhd,bkd->bhqk", qp, k_pe)
        sc = sc.astype(jnp.float32) * scale
        q_chunk = (start + jnp.arange(Q_BLOCK)) // CHUNK
        mask = key_chunk[None, :] <= q_chunk[:, None]
        p = jax.nn.softmax(jnp.where(mask, sc, -jnp.inf), axis=-1).astype(v.dtype)
        return jnp.einsum("bhqk,bkhd->bqhd", p, v)

    out = lax.map(query_block, jnp.arange(s // Q_BLOCK))
    return jnp.swapaxes(out, 0, 1).reshape(b_, s, MLA_WIDTH)


def ssd_chunked(x, dt_raw, bm, cm, a_log, dt_bias, d_skip):
    b_, s, _ = x.shape
    nc = s // CHUNK
    r = SSD_HEADS // SSD_GROUPS
    f32 = jnp.float32
    xh = x.astype(f32).reshape(b_, s, SSD_HEADS, SSD_HEADDIM)
    dt = jax.nn.softplus(dt_raw.astype(f32) + dt_bias.astype(f32))
    da = dt * (-jnp.exp(a_log.astype(f32)))
    xc = (xh * dt[..., None]).reshape(b_, nc, CHUNK, SSD_GROUPS, r, SSD_HEADDIM)
    bc = bm.astype(f32).reshape(b_, nc, CHUNK, SSD_GROUPS, SSD_STATE)
    cc = cm.astype(f32).reshape(b_, nc, CHUNK, SSD_GROUPS, SSD_STATE)
    acs = jnp.cumsum(da.reshape(b_, nc, CHUNK, SSD_GROUPS, r), axis=2)
    acs = jnp.transpose(acs, (0, 1, 3, 4, 2))
    lower = jnp.tril(jnp.ones((CHUNK, CHUNK), dtype=bool))
    lmat = jnp.exp(jnp.where(lower, acs[..., :, None] - acs[..., None, :], -jnp.inf))
    cb = jnp.einsum("bclgn,bcsgn->bcgls", cc, bc)
    y_diag = jnp.einsum("bcgls,bcgrls,bcsgrp->bclgrp", cb, lmat, xc)
    d_start = jnp.exp(acs)
    d_end = jnp.exp(acs[..., -1:] - acs)
    c_decay = jnp.exp(acs[..., -1])

    def step(hs, inp):
        c_i, b_i, x_i, ds_i, de_i, cd_i = inp
        y_off = jnp.einsum("blgn,bgrpn,bgrl->blgrp", c_i, hs, ds_i)
        hs = hs * cd_i[..., None, None] + jnp.einsum("blgn,bgrl,blgrp->bgrpn", b_i, de_i, x_i)
        return hs, y_off

    h0 = jnp.zeros((b_, SSD_GROUPS, r, SSD_HEADDIM, SSD_STATE), f32)
    xs = (jnp.moveaxis(cc, 1, 0), jnp.moveaxis(bc, 1, 0), jnp.moveaxis(xc, 1, 0),
          jnp.moveaxis(d_start, 1, 0), jnp.moveaxis(d_end, 1, 0), jnp.moveaxis(c_decay, 1, 0))
    _, y_off = lax.scan(step, h0, xs)
    y = (y_diag + jnp.moveaxis(y_off, 0, 1)).reshape(b_, s, SSD_HEADS, SSD_HEADDIM)
    y = y + d_skip.astype(f32)[:, None] * xh
    return y.reshape(b_, s, SSD_D_INNER)


def even_mixer(h, positions, w_in, conv_w, a_log, dt_bias, gdn_g,
               q_norm_g, w_uq, kv_norm_g, w_ukv, w_out):
    b_, s, _ = h.shape
    qkv, z, beta_logit, a, c_q, c_kv, k_rope = split_cols(h @ w_in, EVEN_SPLITS)
    qkv = jax.nn.silu(causal_conv(qkv, conv_w))
    q, k, v = split_cols(qkv, (GDN_HEADS * GDN_DK, GDN_HEADS * GDN_DK, GDN_WIDTH))
    o_a = gated_delta_rule(q.reshape(b_, s, GDN_HEADS, GDN_DK), k.reshape(b_, s, GDN_HEADS, GDN_DK),
                           v.reshape(b_, s, GDN_HEADS, GDN_DV), beta_logit, a, a_log, dt_bias)
    o_a = rmsnorm(o_a, gdn_g) * jax.nn.silu(z.astype(jnp.float32).reshape(b_, s, GDN_HEADS, GDN_DV))
    o_b = latent_attention(c_q, c_kv, k_rope, positions, q_norm_g, w_uq, kv_norm_g, w_ukv)
    o = jnp.concatenate([o_a.reshape(b_, s, GDN_WIDTH).astype(h.dtype), o_b.astype(h.dtype)], axis=-1)
    return o @ w_out


def odd_mixer(h, w_in, conv_w, conv_b, a_log, dt_bias, d_skip, norm_g, w_out):
    b_, s, _ = h.shape
    z, xbc, dt_raw = split_cols(h @ w_in, ODD_SPLITS)
    xbc = jax.nn.silu(causal_conv(xbc, conv_w, conv_b))
    xs, bm, cm = split_cols(xbc, (SSD_D_INNER, SSD_GROUPS * SSD_STATE, SSD_GROUPS * SSD_STATE))
    y = ssd_chunked(xs, dt_raw, bm, cm, a_log, dt_bias, d_skip)
    y = y * jax.nn.silu(z.astype(jnp.float32))
    yg = y.reshape(b_, s, SSD_GROUPS, SSD_D_INNER // SSD_GROUPS)
    yg = yg * lax.rsqrt(jnp.mean(yg * yg, axis=-1, keepdims=True) + NORM_EPS)
    y = (yg.reshape(b_, s, SSD_D_INNER) * norm_g.astype(jnp.float32)).astype(h.dtype)
    return y @ w_out


def setup_inputs(seed: int = 0) -> dict:
    key = jax.random.key(seed)
    ks = jax.random.split(key, 27)
    f32 = jnp.float32

    def nrm(k, shape, scale):
        return jax.random.normal(k, shape, f32) * scale

    def gain(k, shape):
        return 1.0 + 0.05 * jax.random.normal(k, shape, f32)

    def a_log_init(k, shape):
        return jnp.log(jax.random.uniform(k, shape, f32, 1.0, 16.0))

    def dt_bias_init(k, shape):
        dt = jnp.exp(jax.random.uniform(k, shape, f32, math.log(1e-3), math.log(1e-1)))
        return dt + jnp.log(-jnp.expm1(-dt))

    positions = jnp.broadcast_to(jnp.arange(SEQ, dtype=jnp.int32), (BATCH, SEQ))
    return {
        "x": nrm(ks[0], (BATCH, SEQ, D_MODEL), 1.0),
        "c": nrm(ks[1], (BATCH, D_MODEL), 1.0),
        "positions": positions,
        "ada_w": nrm(ks[2], (DEPTH, D_MODEL, N_MOD * D_MODEL), 0.5 * D_MODEL ** -0.5),
        "ada_b": nrm(ks[3], (DEPTH, N_MOD * D_MODEL), 0.02),
        "norm_g": gain(ks[4], (DEPTH, 3, D_MODEL)),
        "ffn_w1": nrm(ks[5], (DEPTH, 2, D_MODEL, D_FF), D_MODEL ** -0.5),
        "ffn_w3": nrm(ks[6], (DEPTH, 2, D_MODEL, D_FF), D_MODEL ** -0.5),
        "ffn_w2": nrm(ks[7], (DEPTH, 2, D_FF, D_MODEL), D_FF ** -0.5),
        "ev_w_in": nrm(ks[8], (N_EVEN, D_MODEL, EVEN_IN), D_MODEL ** -0.5),
        "gdn_conv_w": nrm(ks[9], (N_EVEN, CONV_K, GDN_CONV_CH), CONV_K ** -0.5),
        "gdn_A_log": a_log_init(ks[10], (N_EVEN, GDN_HEADS)),
        "gdn_dt_bias": dt_bias_init(ks[11], (N_EVEN, GDN_HEADS)),
        "gdn_norm_g": gain(ks[12], (N_EVEN, GDN_DV)),
        "mla_q_norm_g": gain(ks[13], (N_EVEN, MLA_Q_RANK)),
        "mla_w_uq": nrm(ks[14], (N_EVEN, MLA_Q_RANK, MLA_HEADS, MLA_NOPE + MLA_ROPE), MLA_Q_RANK ** -0.5),
        "mla_kv_norm_g": gain(ks[15], (N_EVEN, MLA_KV_RANK)),
        "mla_w_ukv": nrm(ks[16], (N_EVEN, MLA_KV_RANK, MLA_HEADS, MLA_NOPE + MLA_DV), MLA_KV_RANK ** -0.5),
        "ev_w_out": nrm(ks[17], (N_EVEN, MIX_WIDTH, D_MODEL), MIX_WIDTH ** -0.5),
        "ssd_w_in": nrm(ks[18], (N_ODD, D_MODEL, ODD_IN), D_MODEL ** -0.5),
        "ssd_conv_w": nrm(ks[19], (N_ODD, CONV_K, SSD_CONV_CH), CONV_K ** -0.5),
        "ssd_conv_b": nrm(ks[20], (N_ODD, SSD_CONV_CH), 0.02),
        "ssd_A_log": a_log_init(ks[21], (N_ODD, SSD_HEADS)),
        "ssd_dt_bias": dt_bias_init(ks[22], (N_ODD, SSD_HEADS)),
        "ssd_D": gain(ks[23], (N_ODD, SSD_HEADS)),
        "ssd_norm_g": gain(ks[24], (N_ODD, SSD_D_INNER)),
        "ssd_w_out": nrm(ks[25], (N_ODD, SSD_D_INNER, D_MODEL), SSD_D_INNER ** -0.5),
        "final_g": gain(ks[26], (D_MODEL,)),
    }


def reference(x, c, positions, ada_w, ada_b, norm_g, ffn_w1, ffn_w3, ffn_w2,
              ev_w_in, gdn_conv_w, gdn_A_log, gdn_dt_bias, gdn_norm_g,
              mla_q_norm_g, mla_w_uq, mla_kv_norm_g, mla_w_ukv, ev_w_out,
              ssd_w_in, ssd_conv_w, ssd_conv_b, ssd_A_log, ssd_dt_bias, ssd_D, ssd_norm_g, ssd_w_out,
              final_g):
    b_ = x.shape[0]
    c_act = jax.nn.silu(c)
    for l in range(DEPTH):
        mod = (c_act @ ada_w[l] + ada_b[l]).reshape(b_, 3, 3, D_MODEL)
        shift, scale, gate = mod[:, :, 0], mod[:, :, 1], mod[:, :, 2]
        h = adaln_norm(x, norm_g[l, 0], shift[:, 0], scale[:, 0])
        x = x + 0.5 * gate[:, 0, None] * swiglu(h, ffn_w1[l, 0], ffn_w3[l, 0], ffn_w2[l, 0])
        h = adaln_norm(x, norm_g[l, 1], shift[:, 1], scale[:, 1])
        if l % 2 == 0:
            e = l // 2
            y = even_mixer(h, positions, ev_w_in[e], gdn_conv_w[e], gdn_A_log[e], gdn_dt_bias[e],
                           gdn_norm_g[e], mla_q_norm_g[e], mla_w_uq[e], mla_kv_norm_g[e],
                           mla_w_ukv[e], ev_w_out[e])
        else:
            o = l // 2
            y = odd_mixer(h, ssd_w_in[o], ssd_conv_w[o], ssd_conv_b[o], ssd_A_log[o],
                          ssd_dt_bias[o], ssd_D[o], ssd_norm_g[o], ssd_w_out[o])
        x = x + gate[:, 1, None] * y
        h = adaln_norm(x, norm_g[l, 2], shift[:, 2], scale[:, 2])
        x = x + 0.5 * gate[:, 2, None] * swiglu(h, ffn_w1[l, 1], ffn_w3[l, 1], ffn_w2[l, 1])
    return rmsnorm(x, final_g)
```

```python
import functools
import math

import jax
import jax.numpy as jnp
from jax import lax
from jax.experimental import pallas as pl
from jax.experimental.pallas import tpu as pltpu

F32 = jnp.float32
BF16 = jnp.bfloat16
HIGHEST = lax.Precision.HIGHEST

D_MODEL = 1024
DEPTH = 4
CHUNK = 64
NORM_EPS = 1e-6
CONV_K = 4
D_FF = 2816
N_MOD = 9

GDN_HEADS = 4
GDN_DK = 128
GDN_DV = 128
GDN_WIDTH = GDN_HEADS * GDN_DV
GDN_CONV_CH = 2 * GDN_HEADS * GDN_DK + GDN_WIDTH

MLA_HEADS = 4
MLA_Q_RANK = 384
MLA_KV_RANK = 256
MLA_NOPE = 128
MLA_ROPE = 64
MLA_DV = 128
MLA_WIDTH = MLA_HEADS * MLA_DV
MLA_QK = MLA_NOPE + MLA_ROPE
ROPE_THETA = 10000.0

SSD_D_INNER = 2 * D_MODEL
SSD_HEADDIM = 64
SSD_HEADS = SSD_D_INNER // SSD_HEADDIM
SSD_GROUPS = 4
SSD_STATE = 128
SSD_CONV_CH = SSD_D_INNER + 2 * SSD_GROUPS * SSD_STATE
SSD_GROUP_W = SSD_D_INNER // SSD_GROUPS
SSD_HEADS_PER_GROUP = SSD_HEADS // SSD_GROUPS

LANES = 128
SUBLANES = 8
VMEM_LIMIT = 56 * 1024 * 1024

EV_QKV = 0
EV_Z = GDN_CONV_CH
EV_CQ = EV_Z + GDN_WIDTH
EV_CKV = EV_CQ + MLA_Q_RANK
EV_KR = EV_CKV + MLA_KV_RANK
EV_BA = EV_KR + 2 * MLA_ROPE
EV_IN_P = EV_BA + LANES

OD_Z = 0
OD_XBC = SSD_D_INNER
OD_DT = OD_XBC + SSD_CONV_CH
OD_IN_P = OD_DT + LANES


def _cparams(sem):
    return pltpu.CompilerParams(dimension_semantics=sem, vmem_limit_bytes=VMEM_LIMIT)


def _bdot(a, b):
    return jnp.dot(a.astype(BF16), b.astype(BF16), preferred_element_type=F32)


def _bdot_nt(a, b):
    return lax.dot_general(a.astype(BF16), b.astype(BF16), (((1,), (1,)), ((), ())),
                           preferred_element_type=F32)


def _hdot(a, b):
    return jnp.dot(a, b, precision=HIGHEST, preferred_element_type=F32)


def _split(a):
    hi = a.astype(BF16)
    lo = (a - hi.astype(F32)).astype(BF16)
    return hi, lo


def _dot3(a, b):
    ah, al = _split(a)
    bh, bl = _split(b)
    d = lambda p, q: jnp.dot(p, q, preferred_element_type=F32)
    return d(ah, bh) + (d(ah, bl) + d(al, bh))


def _silu(x):
    return x * jax.nn.sigmoid(x)


def _softplus(x):
    return jnp.maximum(x, 0.0) + jnp.log1p(jnp.exp(-jnp.abs(x)))


def _adaln(x, g, shift, scale):
    ms = jnp.mean(x * x, axis=-1, keepdims=True)
    y = x * lax.rsqrt(ms + NORM_EPS) * g
    return y * (1.0 + scale) + shift


def _mod_kernel(c_ref, w_ref, b_ref, o_ref):
    ca = _silu(c_ref[...])
    cb = jnp.broadcast_to(ca, (D_MODEL, LANES))
    tn = w_ref.shape[2]
    for t in range(tn // LANES):
        sl = slice(t * LANES, (t + 1) * LANES)
        o_ref[0, :, sl] = jnp.sum(w_ref[0, :, sl] * cb, axis=0, keepdims=True) + b_ref[0, :, sl]


def _modulation(c_col, ada_w, ada_b):
    n = N_MOD * D_MODEL
    tn = n // 4
    return pl.pallas_call(
        _mod_kernel,
        grid=(DEPTH, n // tn),
        in_specs=[pl.BlockSpec((D_MODEL, 1), lambda l, j: (0, 0)),
                  pl.BlockSpec((1, D_MODEL, tn), lambda l, j: (l, 0, j)),
                  pl.BlockSpec((1, 1, tn), lambda l, j: (l, 0, j))],
        out_specs=pl.BlockSpec((1, 1, tn), lambda l, j: (l, 0, j)),
        out_shape=jax.ShapeDtypeStruct((DEPTH, 1, n), F32),
        compiler_params=_cparams(("parallel", "parallel")),
        name="modulation",
    )(c_col, ada_w, ada_b.reshape(DEPTH, 1, n))


def _rope_kernel(pos_ref, freq_ref, o_ref):
    ang = pos_ref[...].astype(F32) * freq_ref[...]
    lane = lax.broadcasted_iota(jnp.int32, ang.shape, 1)
    sgn = jnp.where(lane < 3 * (MLA_ROPE // 2), -1.0, 1.0)
    o_ref[...] = jnp.where(lane < MLA_ROPE, jnp.cos(ang), sgn * jnp.sin(ang))


def _rope_table(pos_col):
    s = pos_col.shape[0]
    tm = min(s, 2048)
    half = MLA_ROPE // 2
    inv_freq = ROPE_THETA ** (-jnp.arange(half, dtype=F32) / half)
    freq = jnp.tile(inv_freq, LANES // half).reshape(1, LANES)
    return pl.pallas_call(
        _rope_kernel,
        grid=(s // tm,),
        in_specs=[pl.BlockSpec((tm, 1), lambda i: (i, 0)),
                  pl.BlockSpec((1, LANES), lambda i: (0, 0))],
        out_specs=pl.BlockSpec((tm, LANES), lambda i: (i, 0)),
        out_shape=jax.ShapeDtypeStruct((s, LANES), F32),
        compiler_params=_cparams(("parallel",)),
        name="rope_table",
    )(pos_col, freq)


def _ffn_kernel(x_ref, mod_ref, g_ref, w1_ref, w3_ref, w2_ref, fg_ref, o_ref, a_ref, *, tf, final):
    x = x_ref[...]
    h = _adaln(x, g_ref[...], mod_ref[0:1, :], mod_ref[1:2, :]).astype(BF16)
    for j in range(D_FF // tf):
        sl = slice(j * tf, (j + 1) * tf)
        h1 = jnp.dot(h, w1_ref[:, sl], preferred_element_type=F32)
        h3 = jnp.dot(h, w3_ref[:, sl], preferred_element_type=F32)
        a_ref[:, sl] = (_silu(h1) * h3).astype(BF16)
    y = jnp.dot(a_ref[...], w2_ref[...], preferred_element_type=F32)
    xn = x + (0.5 * mod_ref[2:3, :]) * y
    if final:
        ms = jnp.mean(xn * xn, axis=-1, keepdims=True)
        xn = xn * lax.rsqrt(ms + NORM_EPS) * fg_ref[...]
    o_ref[...] = xn


def _ffn(x, mod3, g, w1, w3, w2, final_g, final):
    s = x.shape[0]
    tm = min(s, 512)
    const = lambda i: (0, 0)
    return pl.pallas_call(
        functools.partial(_ffn_kernel, tf=256, final=final),
        grid=(s // tm,),
        in_specs=[pl.BlockSpec((tm, D_MODEL), lambda i: (i, 0)),
                  pl.BlockSpec((3, D_MODEL), const),
                  pl.BlockSpec((1, D_MODEL), const),
                  pl.BlockSpec((D_MODEL, D_FF), const, pipeline_mode=pl.Buffered(1)),
                  pl.BlockSpec((D_MODEL, D_FF), const, pipeline_mode=pl.Buffered(1)),
                  pl.BlockSpec((D_FF, D_MODEL), const, pipeline_mode=pl.Buffered(1)),
                  pl.BlockSpec((1, D_MODEL), const)],
        out_specs=pl.BlockSpec((tm, D_MODEL), lambda i: (i, 0)),
        out_shape=jax.ShapeDtypeStruct((s, D_MODEL), F32),
        scratch_shapes=[pltpu.VMEM((tm, D_FF), BF16)],
        compiler_params=_cparams(("parallel",)),
        name="ffn_final" if final else "ffn",
    )(x, mod3, g, w1, w3, w2, final_g)


def _causal_conv(p, cw, ext_ref, halo_ref, col0):
    tm, w = p.shape
    cs = slice(col0, col0 + w)
    ext_ref[0:SUBLANES, 0:w] = halo_ref[:, cs]
    ext_ref[SUBLANES:SUBLANES + tm, 0:w] = p
    halo_ref[:, cs] = p[tm - SUBLANES:, :]
    y = p * cw[CONV_K - 1:CONV_K, :]
    for j in range(CONV_K - 1):
        off = SUBLANES - (CONV_K - 1) + j
        y = y + ext_ref[off:off + tm, 0:w] * cw[j:j + 1, :]
    return y


def _even_in_kernel(x_ref, mod_ref, g_ref, w_ref, cw_ref, rope_ref, qg_ref, wuq_ref, kvg_ref, wukv_ref,
                    qkv_ref, z_ref, ba_ref, q_ref, k_ref, v_ref, ext_ref, halo_ref):
    @pl.when(pl.program_id(0) == 0)
    def _():
        halo_ref[...] = jnp.zeros_like(halo_ref)

    x = x_ref[...]
    h = _adaln(x, g_ref[...], mod_ref[0:1, :], mod_ref[1:2, :]).astype(BF16)
    proj = lambda c0, w: jnp.dot(h, w_ref[:, c0:c0 + w], preferred_element_type=F32)

    for part in range(3):
        c0 = part * GDN_WIDTH
        y = _silu(_causal_conv(proj(c0, GDN_WIDTH), cw_ref[:, c0:c0 + GDN_WIDTH], ext_ref, halo_ref, c0))
        if part == 2:
            qkv_ref[:, c0:c0 + GDN_WIDTH] = y
        else:
            post = GDN_DK ** -0.5 if part == 0 else 1.0
            for hd in range(GDN_HEADS):
                yh = y[:, hd * GDN_DK:(hd + 1) * GDN_DK]
                nrm = lax.rsqrt(jnp.sum(yh * yh, axis=-1, keepdims=True) + NORM_EPS)
                qkv_ref[:, c0 + hd * GDN_DK:c0 + (hd + 1) * GDN_DK] = yh * nrm * post
    z_ref[...] = proj(EV_Z, GDN_WIDTH)
    ba_ref[...] = proj(EV_BA, LANES)

    rope = rope_ref[...]
    scale = MLA_QK ** -0.5

    def rot(xx):
        yy = xx * rope
        return yy + pltpu.roll(yy, MLA_ROPE, 1)

    cq = proj(EV_CQ, MLA_Q_RANK)
    cq = cq * lax.rsqrt(jnp.mean(cq * cq, axis=-1, keepdims=True) + NORM_EPS) * qg_ref[...]
    qf = jnp.dot(cq.astype(BF16), wuq_ref[...], preferred_element_type=F32)
    ckv = proj(EV_CKV, MLA_KV_RANK)
    ckv = ckv * lax.rsqrt(jnp.mean(ckv * ckv, axis=-1, keepdims=True) + NORM_EPS) * kvg_ref[...]
    kvf = jnp.dot(ckv.astype(BF16), wukv_ref[...], preferred_element_type=F32)
    kpe = rot(proj(EV_KR, 2 * MLA_ROPE))[:, 0:MLA_ROPE].astype(BF16)
    for hd in range(MLA_HEADS):
        b0 = hd * 2 * LANES
        q_ref[hd, :, 0:MLA_NOPE] = (qf[:, b0:b0 + MLA_NOPE] * scale).astype(BF16)
        qpe = rot(qf[:, b0 + LANES:b0 + 2 * LANES]) * scale
        q_ref[hd, :, MLA_NOPE:MLA_QK] = qpe[:, 0:MLA_ROPE].astype(BF16)
        k_ref[hd, :, 0:MLA_NOPE] = kvf[:, b0:b0 + MLA_NOPE].astype(BF16)
        k_ref[hd, :, MLA_NOPE:MLA_QK] = kpe
        v_ref[hd] = kvf[:, b0 + LANES:b0 + 2 * LANES].astype(BF16)


def _even_in(x, mod3, g, w_in_p, conv_w, rope, qg, wuq_p, kvg, wukv_p):
    s = x.shape[0]
    tm = min(s, 512)
    const = lambda i: (0, 0)
    row = lambda i: (i, 0)
    hrow = lambda i: (0, i, 0)
    return pl.pallas_call(
        _even_in_kernel,
        grid=(s // tm,),
        in_specs=[pl.BlockSpec((tm, D_MODEL), row),
                  pl.BlockSpec((3, D_MODEL), const),
                  pl.BlockSpec((1, D_MODEL), const),
                  pl.BlockSpec((D_MODEL, EV_IN_P), const),
                  pl.BlockSpec((CONV_K, GDN_CONV_CH), const),
                  pl.BlockSpec((tm, LANES), row),
                  pl.BlockSpec((1, MLA_Q_RANK), const),
                  pl.BlockSpec((MLA_Q_RANK, MLA_HEADS * 2 * LANES), const),
                  pl.BlockSpec((1, MLA_KV_RANK), const),
                  pl.BlockSpec((MLA_KV_RANK, MLA_HEADS * 2 * LANES), const)],
        out_specs=[pl.BlockSpec((tm, GDN_CONV_CH), row),
                   pl.BlockSpec((tm, GDN_WIDTH), row),
                   pl.BlockSpec((tm, LANES), row),
                   pl.BlockSpec((MLA_HEADS, tm, MLA_QK), hrow),
                   pl.BlockSpec((MLA_HEADS, tm, MLA_QK), hrow),
                   pl.BlockSpec((MLA_HEADS, tm, MLA_DV), hrow)],
        out_shape=[jax.ShapeDtypeStruct((s, GDN_CONV_CH), F32),
                   jax.ShapeDtypeStruct((s, GDN_WIDTH), F32),
                   jax.ShapeDtypeStruct((s, LANES), F32),
                   jax.ShapeDtypeStruct((MLA_HEADS, s, MLA_QK), BF16),
                   jax.ShapeDtypeStruct((MLA_HEADS, s, MLA_QK), BF16),
                   jax.ShapeDtypeStruct((MLA_HEADS, s, MLA_DV), BF16)],
        scratch_shapes=[pltpu.VMEM((tm + SUBLANES, GDN_WIDTH), F32),
                        pltpu.VMEM((SUBLANES, GDN_CONV_CH), F32)],
        compiler_params=_cparams(("arbitrary",)),
        name="even_in",
    )(x, mod3, g, w_in_p, conv_w, rope, qg, wuq_p, kvg, wukv_p)


GDN_TB = 256


def _gdn_kernel(qkv_ref, ba_ref, z_ref, alog_ref, dtb_ref, gn_ref, o_ref, s_ref):
    tb = GDN_TB
    nch = tb // CHUNK

    @pl.when(pl.program_id(0) == 0)
    def _():
        s_ref[...] = jnp.zeros_like(s_ref)

    ba = ba_ref[...]
    lane = lax.broadcasted_iota(jnp.int32, (tb, LANES), 1)
    beta_all = jax.nn.sigmoid(ba)
    g_all = jnp.where((lane >= GDN_HEADS) & (lane < 2 * GDN_HEADS),
                      -jnp.exp(alog_ref[...]) * _softplus(ba + dtb_ref[...]), 0.0)
    ri = lax.broadcasted_iota(jnp.int32, (tb, tb), 0)
    ci = lax.broadcasted_iota(jnp.int32, (tb, tb), 1)
    same = (ri // CHUNK) == (ci // CHUNK)
    lower = same & (ci <= ri)
    strict = same & (ci < ri)
    eye = (ri == ci).astype(F32)
    gc_all = _hdot(lower.astype(F32), g_all)
    gc_t = gc_all.T
    gn = gn_ref[...]

    pre = []
    for hd in range(GDN_HEADS):
        gcol = gc_all[:, GDN_HEADS + hd:GDN_HEADS + hd + 1]
        grow = gc_t[GDN_HEADS + hd:GDN_HEADS + hd + 1, :]
        decay = jnp.where(lower, jnp.exp(jnp.where(lower, gcol - grow, 0.0)), 0.0)
        beta = beta_all[:, hd:hd + 1]
        q = qkv_ref[:, hd * GDN_DK:(hd + 1) * GDN_DK]
        k = qkv_ref[:, GDN_WIDTH + hd * GDN_DK:GDN_WIDTH + (hd + 1) * GDN_DK]
        v = qkv_ref[:, 2 * GDN_WIDTH + hd * GDN_DV:2 * GDN_WIDTH + (hd + 1) * GDN_DV]
        kb = k * beta
        egc = jnp.exp(gcol)
        pw = jnp.where(strict, -(_bdot_nt(kb, k) * decay), 0.0)
        tinv = eye + pw
        for _ in range(5):
            pw = _dot3(pw, pw)
            tinv = tinv + _dot3(tinv, pw)
        sol = _dot3(tinv, jnp.concatenate([v * beta, kb * egc], axis=1))
        u, w = sol[:, 0:GDN_DV], sol[:, GDN_DV:]
        attn = _bdot_nt(q, k) * decay
        qs = q * egc
        gl_row = jnp.where(same & ((ci % CHUNK) == CHUNK - 1), grow, 0.0)
        gl = jnp.sum(gl_row, axis=-1, keepdims=True)
        kend_t = (k * jnp.exp(gl - gcol)).T
        pre.append((u, w, attn, qs, kend_t, jnp.exp(gl)))

    col_chunk = lax.broadcasted_iota(jnp.int32, (GDN_DK, tb), 1) // CHUNK
    for c in range(nch):
        rs = slice(c * CHUNK, (c + 1) * CHUNK)
        for hd in range(GDN_HEADS):
            u, w, attn, qs, kend_t, egl = pre[hd]
            st = s_ref[hd]
            ws = _bdot(jnp.concatenate([w[rs], qs[rs]], axis=0), st)
            v_new = u[rs] - ws[0:CHUNK]
            parts = [jnp.zeros((CHUNK, GDN_DV), F32)] * nch
            parts[c] = v_new
            v_full = jnp.concatenate(parts, axis=0)
            o = ws[CHUNK:] + _bdot(attn[rs], v_full)
            kt = jnp.where(col_chunk == c, kend_t, 0.0)
            s_ref[hd] = st * egl[c * CHUNK:c * CHUNK + 1, :] + _bdot(kt, v_full)
            ms = jnp.mean(o * o, axis=-1, keepdims=True)
            zz = z_ref[rs, hd * GDN_DV:(hd + 1) * GDN_DV]
            o_ref[rs, hd * GDN_DV:(hd + 1) * GDN_DV] = (o * lax.rsqrt(ms + NORM_EPS) * gn * _silu(zz)).astype(BF16)


def _gdn(qkv, ba, z, alog_row, dtb_row, gn):
    s = qkv.shape[0]
    tb = GDN_TB
    const = lambda i: (0, 0)
    row = lambda i: (i, 0)
    return pl.pallas_call(
        _gdn_kernel,
        grid=(s // tb,),
        in_specs=[pl.BlockSpec((tb, GDN_CONV_CH), row),
                  pl.BlockSpec((tb, LANES), row),
                  pl.BlockSpec((tb, GDN_WIDTH), row),
                  pl.BlockSpec((1, LANES), const),
                  pl.BlockSpec((1, LANES), const),
                  pl.BlockSpec((1, GDN_DV), const)],
        out_specs=pl.BlockSpec((tb, GDN_WIDTH), row),
        out_shape=jax.ShapeDtypeStruct((s, GDN_WIDTH), BF16),
        scratch_shapes=[pltpu.VMEM((GDN_HEADS, GDN_DK, GDN_DV), F32)],
        compiler_params=_cparams(("arbitrary",)),
        name="gdn",
    )(qkv, ba, z, alog_row, dtb_row, gn)


def _flash_kernel(q_ref, k_ref, v_ref, o_ref, m_ref, l_ref, acc_ref, *, tq, tk):
    qi = pl.program_id(1)
    ki = pl.program_id(2)
    last = ((qi + 1) * tq - 1) // tk

    @pl.when(ki == 0)
    def _():
        m_ref[...] = jnp.full_like(m_ref, -jnp.inf)
        l_ref[...] = jnp.zeros_like(l_ref)
        acc_ref[...] = jnp.zeros_like(acc_ref)

    def step(masked):
        s = lax.dot_general(q_ref[0], k_ref[0], (((1,), (1,)), ((), ())), preferred_element_type=F32)
        if masked:
            qc = (qi * tq + lax.broadcasted_iota(jnp.int32, (tq, tk), 0)) // CHUNK
            kc = (ki * tk + lax.broadcasted_iota(jnp.int32, (tq, tk), 1)) // CHUNK
            s = jnp.where(kc <= qc, s, -jnp.inf)
        m_old = m_ref[...]
        m_new = jnp.maximum(m_old, jnp.max(s, axis=-1, keepdims=True))
        alpha = jnp.exp(m_old - m_new)
        p = jnp.exp(s - m_new)
        l_ref[...] = l_ref[...] * alpha + jnp.sum(p, axis=-1, keepdims=True)
        acc_ref[...] = acc_ref[...] * alpha + jnp.dot(p.astype(BF16), v_ref[0], preferred_element_type=F32)
        m_ref[...] = m_new

    full = ((ki + 1) * tk) <= (qi * tq + CHUNK)

    @pl.when(full)
    def _():
        step(False)

    @pl.when(jnp.logical_not(full) & (ki <= last))
    def _():
        step(True)

    @pl.when(ki == last)
    def _():
        o_ref[...] = (acc_ref[...] / l_ref[...]).astype(o_ref.dtype)


def _flash(q, k, v):
    nh, s, _ = q.shape
    tq = min(s, 1024)
    tk = min(s, 512)

    def kv_map(h, qi, ki):
        return (h, jnp.minimum(ki, ((qi + 1) * tq - 1) // tk), 0)

    return pl.pallas_call(
        functools.partial(_flash_kernel, tq=tq, tk=tk),
        grid=(nh, s // tq, s // tk),
        in_specs=[pl.BlockSpec((1, tq, MLA_QK), lambda h, qi, ki: (h, qi, 0)),
                  pl.BlockSpec((1, tk, MLA_QK), kv_map),
                  pl.BlockSpec((1, tk, MLA_DV), kv_map)],
        out_specs=pl.BlockSpec((tq, MLA_DV), lambda h, qi, ki: (qi, h)),
        out_shape=jax.ShapeDtypeStruct((s, MLA_WIDTH), BF16),
        scratch_shapes=[pltpu.VMEM((tq, 1), F32), pltpu.VMEM((tq, 1), F32), pltpu.VMEM((tq, MLA_DV), F32)],
        compiler_params=_cparams(("parallel", "parallel", "arbitrary")),
        name="mla_flash",
    )(q, k, v)


def _odd_in_kernel(x_ref, mod_ref, g_ref, w_ref, cw_ref, cb_ref, z_ref, xbc_ref, dt_ref, ext_ref, halo_ref, *, tc):
    @pl.when(pl.program_id(0) == 0)
    def _():
        halo_ref[...] = jnp.zeros_like(halo_ref)

    x = x_ref[...]
    h = _adaln(x, g_ref[...], mod_ref[0:1, :], mod_ref[1:2, :]).astype(BF16)
    proj = lambda c0, w: jnp.dot(h, w_ref[:, c0:c0 + w], preferred_element_type=F32)
    for j in range(SSD_D_INNER // tc):
        z_ref[:, j * tc:(j + 1) * tc] = proj(OD_Z + j * tc, tc)
    for j in range(SSD_CONV_CH // tc):
        c0 = j * tc
        y = _causal_conv(proj(OD_XBC + c0, tc), cw_ref[:, c0:c0 + tc], ext_ref, halo_ref, c0)
        xbc_ref[:, c0:c0 + tc] = _silu(y + cb_ref[:, c0:c0 + tc])
    dt_ref[...] = proj(OD_DT, LANES)


def _odd_in(x, mod3, g, w_in_p, conv_w, conv_b):
    s = x.shape[0]
    tm = min(s, 512)
    tc = 512
    const = lambda i: (0, 0)
    row = lambda i: (i, 0)
    return pl.pallas_call(
        functools.partial(_odd_in_kernel, tc=tc),
        grid=(s // tm,),
        in_specs=[pl.BlockSpec((tm, D_MODEL), row),
                  pl.BlockSpec((3, D_MODEL), const),
                  pl.BlockSpec((1, D_MODEL), const),
                  pl.BlockSpec((D_MODEL, OD_IN_P), const),
                  pl.BlockSpec((CONV_K, SSD_CONV_CH), const),
                  pl.BlockSpec((1, SSD_CONV_CH), const)],
        out_specs=[pl.BlockSpec((tm, SSD_D_INNER), row),
                   pl.BlockSpec((tm, SSD_CONV_CH), row),
                   pl.BlockSpec((tm, LANES), row)],
        out_shape=[jax.ShapeDtypeStruct((s, SSD_D_INNER), F32),
                   jax.ShapeDtypeStruct((s, SSD_CONV_CH), F32),
                   jax.ShapeDtypeStruct((s, LANES), F32)],
        scratch_shapes=[pltpu.VMEM((tm + SUBLANES, tc), F32),
                        pltpu.VMEM((SUBLANES, SSD_CONV_CH), F32)],
        compiler_params=_cparams(("arbitrary",)),
        name="odd_in",
    )(x, mod3, g, w_in_p, conv_w, conv_b)


SSD_TB = 256


def _ssd_kernel(xbc_ref, dt_ref, z_ref, alog_ref, dtb_ref, dskip_ref, ng_ref, o_ref, hs_ref):
    tb = SSD_TB

    @pl.when(pl.program_id(0) == 0)
    def _():
        hs_ref[...] = jnp.zeros_like(hs_ref)

    lane = lax.broadcasted_iota(jnp.int32, (tb, LANES), 1)
    dt = jnp.where(lane < SSD_HEADS, _softplus(dt_ref[...] + dtb_ref[...]), 0.0)
    da = dt * (-jnp.exp(alog_ref[...]))
    ri = lax.broadcasted_iota(jnp.int32, (tb, tb), 0)
    ci = lax.broadcasted_iota(jnp.int32, (tb, tb), 1)
    lower = ci <= ri
    acs = _hdot(lower.astype(F32), da)
    acs_t = acs.T
    a_last = acs[tb - 1:tb, :]

    er = lax.broadcasted_iota(jnp.int32, (LANES, SSD_GROUP_W), 0)
    ec = lax.broadcasted_iota(jnp.int32, (LANES, SSD_GROUP_W), 1) // SSD_HEADDIM

    for g in range(SSD_GROUPS):
        expand = (er == ec + g * SSD_HEADS_PER_GROUP).astype(F32)
        xs = xbc_ref[:, g * SSD_GROUP_W:(g + 1) * SSD_GROUP_W]
        bg = xbc_ref[:, SSD_D_INNER + g * SSD_STATE:SSD_D_INNER + (g + 1) * SSD_STATE]
        cg = xbc_ref[:, SSD_D_INNER + (SSD_GROUPS + g) * SSD_STATE:SSD_D_INNER + (SSD_GROUPS + g + 1) * SSD_STATE]
        xdt = xs * _hdot(dt, expand)
        d_start = jnp.exp(_hdot(acs, expand))
        d_end = jnp.exp(_hdot(a_last - acs, expand))
        c_decay = jnp.exp(_hdot(jnp.broadcast_to(a_last, (SUBLANES, LANES)), expand))[0:1, :]
        cb = _bdot_nt(cg, bg)
        hs = hs_ref[g]
        y = _bdot(cg, hs) * d_start
        lane512 = lax.broadcasted_iota(jnp.int32, (tb, LANES), 1)
        pieces = []
        for pr in range(SSD_HEADS_PER_GROUP // 2):
            xpair = xdt[:, pr * LANES:(pr + 1) * LANES]
            acc = None
            for half in range(2):
                hd = g * SSD_HEADS_PER_GROUP + 2 * pr + half
                col = acs[:, hd:hd + 1]
                rowv = acs_t[hd:hd + 1, :]
                lmat = jnp.where(lower, jnp.exp(jnp.where(lower, col - rowv, 0.0)), 0.0)
                keep = (lane512 < SSD_HEADDIM) if half == 0 else (lane512 >= SSD_HEADDIM)
                xh = jnp.where(keep, xpair, 0.0)
                t = _bdot(cb * lmat, xh)
                acc = t if acc is None else acc + t
            pieces.append(acc)
        y = y + jnp.concatenate(pieces, axis=1)
        hs_ref[g] = hs * c_decay + _bdot(bg.T, xdt * d_end)
        y = y + dskip_ref[:, g * SSD_GROUP_W:(g + 1) * SSD_GROUP_W] * xs
        y = y * _silu(z_ref[:, g * SSD_GROUP_W:(g + 1) * SSD_GROUP_W])
        ms = jnp.mean(y * y, axis=-1, keepdims=True)
        y = y * lax.rsqrt(ms + NORM_EPS) * ng_ref[:, g * SSD_GROUP_W:(g + 1) * SSD_GROUP_W]
        o_ref[:, g * SSD_GROUP_W:(g + 1) * SSD_GROUP_W] = y.astype(BF16)


def _ssd(xbc, dt, z, alog_row, dtb_row, dskip_row, ng):
    s = xbc.shape[0]
    tb = SSD_TB
    const = lambda i: (0, 0)
    row = lambda i: (i, 0)
    return pl.pallas_call(
        _ssd_kernel,
        grid=(s // tb,),
        in_specs=[pl.BlockSpec((tb, SSD_CONV_CH), row),
                  pl.BlockSpec((tb, LANES), row),
                  pl.BlockSpec((tb, SSD_D_INNER), row),
                  pl.BlockSpec((1, LANES), const),
                  pl.BlockSpec((1, LANES), const),
                  pl.BlockSpec((1, SSD_D_INNER), const),
                  pl.BlockSpec((1, SSD_D_INNER), const)],
        out_specs=pl.BlockSpec((tb, SSD_D_INNER), row),
        out_shape=jax.ShapeDtypeStruct((s, SSD_D_INNER), BF16),
        scratch_shapes=[pltpu.VMEM((SSD_GROUPS, SSD_STATE, SSD_GROUP_W), F32)],
        compiler_params=_cparams(("arbitrary",)),
        name="ssd",
    )(xbc, dt, z, alog_row, dtb_row, dskip_row, ng)


def _out_kernel(*refs, n_in):
    x_ref, mod_ref = refs[0], refs[1]
    a_refs = refs[2:2 + n_in]
    w_ref = refs[2 + n_in]
    o_ref = refs[3 + n_in]
    y = None
    k0 = 0
    for a_ref in a_refs:
        kw = a_ref.shape[1]
        t = jnp.dot(a_ref[...], w_ref[k0:k0 + kw, :], preferred_element_type=F32)
        y = t if y is None else y + t
        k0 += kw
    o_ref[...] = x_ref[...] + mod_ref[2:3, :] * y


def _out_proj(x, mod3, acts, w):
    s = x.shape[0]
    tm = min(s, 512)
    const = lambda i: (0, 0)
    row = lambda i: (i, 0)
    return pl.pallas_call(
        functools.partial(_out_kernel, n_in=len(acts)),
        grid=(s // tm,),
        in_specs=[pl.BlockSpec((tm, D_MODEL), row), pl.BlockSpec((3, D_MODEL), const)]
                 + [pl.BlockSpec((tm, a.shape[1]), row) for a in acts]
                 + [pl.BlockSpec(w.shape, const)],
        out_specs=pl.BlockSpec((tm, D_MODEL), row),
        out_shape=jax.ShapeDtypeStruct((s, D_MODEL), F32),
        compiler_params=_cparams(("parallel",)),
        name="out_proj",
    )(x, mod3, *acts, w)


def _pad_lanes(v, lane0):
    return jnp.zeros((1, LANES), F32).at[0, lane0:lane0 + v.shape[0]].set(v.astype(F32))


def _swap_halves(w):
    half = w.shape[-1] // 2
    return jnp.concatenate([w[..., half:], w[..., :half]], axis=-1)


def _even_w_in(w):
    qkv, z, beta, a, cq, ckv, kr = jnp.split(
        w, [GDN_CONV_CH, GDN_CONV_CH + GDN_WIDTH, GDN_CONV_CH + GDN_WIDTH + GDN_HEADS,
            GDN_CONV_CH + GDN_WIDTH + 2 * GDN_HEADS, GDN_CONV_CH + GDN_WIDTH + 2 * GDN_HEADS + MLA_Q_RANK,
            GDN_CONV_CH + GDN_WIDTH + 2 * GDN_HEADS + MLA_Q_RANK + MLA_KV_RANK], axis=1)
    pad = jnp.zeros((D_MODEL, LANES - 2 * GDN_HEADS), w.dtype)
    return jnp.concatenate([qkv, z, cq, ckv, kr, _swap_halves(kr), beta, a, pad], axis=1).astype(BF16)


def _even_w_uq(w):
    pe = w[..., MLA_NOPE:]
    return jnp.concatenate([w, _swap_halves(pe)], axis=-1).reshape(MLA_Q_RANK, MLA_HEADS * 2 * LANES).astype(BF16)


def _odd_w_in(w):
    pad = jnp.zeros((D_MODEL, LANES - SSD_HEADS), w.dtype)
    return jnp.concatenate([w, pad], axis=1).astype(BF16)


def kernel(x, c, positions, ada_w, ada_b, norm_g, ffn_w1, ffn_w3, ffn_w2, ev_w_in, gdn_conv_w, gdn_A_log, gdn_dt_bias, gdn_norm_g, mla_q_norm_g, mla_w_uq, mla_kv_norm_g, mla_w_ukv, ev_w_out, ssd_w_in, ssd_conv_w, ssd_conv_b, ssd_A_log, ssd_dt_bias, ssd_D, ssd_norm_g, ssd_w_out, final_g):
    b, s, d = x.shape
    assert b == 1 and d == D_MODEL and s % 256 == 0
    xs = x.reshape(s, d)
    mod = _modulation(c.reshape(d, 1), ada_w, ada_b).reshape(DEPTH, 3, 3, d)
    rope = _rope_table(positions.reshape(s, 1))
    fg = final_g.reshape(1, d)
    for l in range(DEPTH):
        last = l == DEPTH - 1
        xs = _ffn(xs, mod[l, 0], norm_g[l, 0].reshape(1, d), ffn_w1[l, 0].astype(BF16), ffn_w3[l, 0].astype(BF16),
                  ffn_w2[l, 0].astype(BF16), fg, False)
        g1 = norm_g[l, 1].reshape(1, d)
        if l % 2 == 0:
            e = l // 2
            qkv, z, ba, q, k, v = _even_in(
                xs, mod[l, 1], g1, _even_w_in(ev_w_in[e]), gdn_conv_w[e], rope,
                mla_q_norm_g[e].reshape(1, -1), _even_w_uq(mla_w_uq[e]),
                mla_kv_norm_g[e].reshape(1, -1),
                mla_w_ukv[e].reshape(MLA_KV_RANK, MLA_HEADS * 2 * LANES).astype(BF16))
            o_a = _gdn(qkv, ba, z, _pad_lanes(gdn_A_log[e], GDN_HEADS), _pad_lanes(gdn_dt_bias[e], GDN_HEADS),
                       gdn_norm_g[e].reshape(1, -1))
            o_b = _flash(q, k, v)
            xs = _out_proj(xs, mod[l, 1], [o_a, o_b], ev_w_out[e].astype(BF16))
        else:
            o = l // 2
            z, xbc, dt = _odd_in(xs, mod[l, 1], g1, _odd_w_in(ssd_w_in[o]), ssd_conv_w[o],
                                 ssd_conv_b[o].reshape(1, -1))
            y = _ssd(xbc, dt, z, _pad_lanes(ssd_A_log[o], 0), _pad_lanes(ssd_dt_bias[o], 0),
                     jnp.repeat(ssd_D[o].astype(F32), SSD_HEADDIM).reshape(1, -1), ssd_norm_g[o].reshape(1, -1))
            xs = _out_proj(xs, mod[l, 1], [y], ssd_w_out[o].astype(BF16))
        xs = _ffn(xs, mod[l, 2], norm_g[l, 2].reshape(1, d), ffn_w1[l, 1].astype(BF16), ffn_w3[l, 1].astype(BF16),
                  ffn_w2[l, 1].astype(BF16), fg, last)
    return xs.reshape(b, s, d)
```

```python
import functools
import math

import jax
import jax.numpy as jnp
from jax import lax
from jax.experimental import pallas as pl
from jax.experimental.pallas import tpu as pltpu

F32 = jnp.float32
BF16 = jnp.bfloat16

D_MODEL = 1024
DEPTH = 4
CHUNK = 64
NORM_EPS = 1e-6
CONV_K = 4
D_FF = 2816
N_MOD = 9

GDN_HEADS = 4
GDN_DK = 128
GDN_DV = 128
GDN_WIDTH = GDN_HEADS * GDN_DV
GDN_CONV_CH = 2 * GDN_HEADS * GDN_DK + GDN_WIDTH

MLA_HEADS = 4
MLA_Q_RANK = 384
MLA_KV_RANK = 256
MLA_NOPE = 128
MLA_ROPE = 64
MLA_DV = 128
MLA_WIDTH = MLA_HEADS * MLA_DV
MLA_QK = MLA_NOPE + MLA_ROPE
ROPE_THETA = 10000.0

SSD_D_INNER = 2 * D_MODEL
SSD_HEADDIM = 64
SSD_HEADS = SSD_D_INNER // SSD_HEADDIM
SSD_GROUPS = 4
SSD_STATE = 128
SSD_CONV_CH = SSD_D_INNER + 2 * SSD_GROUPS * SSD_STATE
SSD_GROUP_W = SSD_D_INNER // SSD_GROUPS
SSD_HEADS_PER_GROUP = SSD_HEADS // SSD_GROUPS

LANES = 128
SUBLANES = 8
VMEM_LIMIT = 56 * 1024 * 1024

EV_QKV = 0
EV_Z = GDN_CONV_CH
EV_CQ = EV_Z + GDN_WIDTH
EV_CKV = EV_CQ + MLA_Q_RANK
EV_KR = EV_CKV + MLA_KV_RANK
EV_BA = EV_KR + 2 * MLA_ROPE
EV_IN_P = EV_BA + LANES

OD_Z = 0
OD_XBC = SSD_D_INNER
OD_DT = OD_XBC + SSD_CONV_CH
OD_IN_P = OD_DT + LANES


def _cparams(sem):
    return pltpu.CompilerParams(dimension_semantics=sem, vmem_limit_bytes=VMEM_LIMIT)


def _bdot(a, b):
    return jnp.dot(a.astype(BF16), b.astype(BF16), preferred_element_type=F32)


def _bdot_nt(a, b):
    return lax.dot_general(a.astype(BF16), b.astype(BF16), (((1,), (1,)), ((), ())),
                           preferred_element_type=F32)


def _mask_dot(mask, x):
    m = jnp.where(mask, 1.0, 0.0).astype(BF16)
    out = None
    r = x
    for _ in range(3):
        part = r.astype(BF16)
        r = r - part.astype(F32)
        t = jnp.dot(m, part, preferred_element_type=F32)
        out = t if out is None else out + t
    return out


def _silu(x):
    return x * jax.nn.sigmoid(x)


def _softplus(x):
    return jnp.maximum(x, 0.0) + jnp.log1p(jnp.exp(-jnp.abs(x)))


def _adaln(x, g, shift, scale):
    ms = jnp.mean(x * x, axis=-1, keepdims=True)
    y = x * lax.rsqrt(ms + NORM_EPS) * g
    return y * (1.0 + scale) + shift


def _mod_kernel(c_ref, w_ref, b_ref, o_ref):
    ca = _silu(c_ref[...])
    cb = jnp.broadcast_to(ca, (D_MODEL, LANES))
    tn = w_ref.shape[2]
    for t in range(tn // LANES):
        sl = slice(t * LANES, (t + 1) * LANES)
        o_ref[0, :, sl] = jnp.sum(w_ref[0, :, sl] * cb, axis=0, keepdims=True) + b_ref[0, :, sl]


def _modulation(c_col, ada_w, ada_b):
    n = N_MOD * D_MODEL
    tn = n // 4
    return pl.pallas_call(
        _mod_kernel,
        grid=(DEPTH, n // tn),
        in_specs=[pl.BlockSpec((D_MODEL, 1), lambda l, j: (0, 0)),
                  pl.BlockSpec((1, D_MODEL, tn), lambda l, j: (l, 0, j)),
                  pl.BlockSpec((1, 1, tn), lambda l, j: (l, 0, j))],
        out_specs=pl.BlockSpec((1, 1, tn), lambda l, j: (l, 0, j)),
        out_shape=jax.ShapeDtypeStruct((DEPTH, 1, n), F32),
        compiler_params=_cparams(("parallel", "parallel")),
        name="modulation",
    )(c_col, ada_w, ada_b.reshape(DEPTH, 1, n))


def _rope_kernel(pos_ref, freq_ref, o_ref):
    ang = pos_ref[...].astype(F32) * freq_ref[...]
    lane = lax.broadcasted_iota(jnp.int32, ang.shape, 1)
    sgn = jnp.where(lane < 3 * (MLA_ROPE // 2), -1.0, 1.0)
    o_ref[...] = jnp.where(lane < MLA_ROPE, jnp.cos(ang), sgn * jnp.sin(ang))


def _rope_table(pos_col):
    s = pos_col.shape[0]
    tm = min(s, 2048)
    half = MLA_ROPE // 2
    inv_freq = ROPE_THETA ** (-jnp.arange(half, dtype=F32) / half)
    freq = jnp.tile(inv_freq, LANES // half).reshape(1, LANES)
    return pl.pallas_call(
        _rope_kernel,
        grid=(s // tm,),
        in_specs=[pl.BlockSpec((tm, 1), lambda i: (i, 0)),
                  pl.BlockSpec((1, LANES), lambda i: (0, 0))],
        out_specs=pl.BlockSpec((tm, LANES), lambda i: (i, 0)),
        out_shape=jax.ShapeDtypeStruct((s, LANES), F32),
        compiler_params=_cparams(("parallel",)),
        name="rope_table",
    )(pos_col, freq)


def _ffn_kernel(x_ref, mod_ref, g_ref, w1_ref, w3_ref, w2_ref, fg_ref, o_ref, a_ref, *, tf, final):
    x = x_ref[...]
    h = _adaln(x, g_ref[...], mod_ref[0:1, :], mod_ref[1:2, :]).astype(BF16)
    for j in range(D_FF // tf):
        sl = slice(j * tf, (j + 1) * tf)
        h1 = jnp.dot(h, w1_ref[:, sl], preferred_element_type=F32)
        h3 = jnp.dot(h, w3_ref[:, sl], preferred_element_type=F32)
        a_ref[:, sl] = (_silu(h1) * h3).astype(BF16)
    y = jnp.dot(a_ref[...], w2_ref[...], preferred_element_type=F32)
    xn = x + (0.5 * mod_ref[2:3, :]) * y
    if final:
        ms = jnp.mean(xn * xn, axis=-1, keepdims=True)
        xn = xn * lax.rsqrt(ms + NORM_EPS) * fg_ref[...]
    o_ref[...] = xn


def _ffn(x, mod3, g, w1, w3, w2, final_g, final):
    s = x.shape[0]
    tm = min(s, 512)
    const = lambda i: (0, 0)
    return pl.pallas_call(
        functools.partial(_ffn_kernel, tf=256, final=final),
        grid=(s // tm,),
        in_specs=[pl.BlockSpec((tm, D_MODEL), lambda i: (i, 0)),
                  pl.BlockSpec((3, D_MODEL), const),
                  pl.BlockSpec((1, D_MODEL), const),
                  pl.BlockSpec((D_MODEL, D_FF), const, pipeline_mode=pl.Buffered(1)),
                  pl.BlockSpec((D_MODEL, D_FF), const, pipeline_mode=pl.Buffered(1)),
                  pl.BlockSpec((D_FF, D_MODEL), const, pipeline_mode=pl.Buffered(1)),
                  pl.BlockSpec((1, D_MODEL), const)],
        out_specs=pl.BlockSpec((tm, D_MODEL), lambda i: (i, 0)),
        out_shape=jax.ShapeDtypeStruct((s, D_MODEL), F32),
        scratch_shapes=[pltpu.VMEM((tm, D_FF), BF16)],
        compiler_params=_cparams(("parallel",)),
        name="ffn_final" if final else "ffn",
    )(x, mod3, g, w1, w3, w2, final_g)


def _causal_conv(p, cw, ext_ref, halo_ref, col0):
    tm, w = p.shape
    cs = slice(col0, col0 + w)
    ext_ref[0:SUBLANES, 0:w] = halo_ref[:, cs]
    ext_ref[SUBLANES:SUBLANES + tm, 0:w] = p
    halo_ref[:, cs] = p[tm - SUBLANES:, :]
    y = p * cw[CONV_K - 1:CONV_K, :]
    for j in range(CONV_K - 1):
        off = SUBLANES - (CONV_K - 1) + j
        y = y + ext_ref[off:off + tm, 0:w] * cw[j:j + 1, :]
    return y


def _even_in_kernel(x_ref, mod_ref, g_ref, w_ref, cw_ref, rope_ref, qg_ref, wuq_ref, kvg_ref, wukv_ref,
                    qkv_ref, z_ref, ba_ref, q_ref, k_ref, v_ref, ext_ref, halo_ref):
    @pl.when(pl.program_id(0) == 0)
    def _():
        halo_ref[...] = jnp.zeros_like(halo_ref)

    x = x_ref[...]
    h = _adaln(x, g_ref[...], mod_ref[0:1, :], mod_ref[1:2, :]).astype(BF16)
    proj = lambda c0, w: jnp.dot(h, w_ref[:, c0:c0 + w], preferred_element_type=F32)

    for part in range(3):
        c0 = part * GDN_WIDTH
        y = _silu(_causal_conv(proj(c0, GDN_WIDTH), cw_ref[:, c0:c0 + GDN_WIDTH], ext_ref, halo_ref, c0))
        if part == 2:
            qkv_ref[:, c0:c0 + GDN_WIDTH] = y
        else:
            post = GDN_DK ** -0.5 if part == 0 else 1.0
            for hd in range(GDN_HEADS):
                yh = y[:, hd * GDN_DK:(hd + 1) * GDN_DK]
                nrm = lax.rsqrt(jnp.sum(yh * yh, axis=-1, keepdims=True) + NORM_EPS)
                qkv_ref[:, c0 + hd * GDN_DK:c0 + (hd + 1) * GDN_DK] = yh * nrm * post
    z_ref[...] = proj(EV_Z, GDN_WIDTH)
    ba_ref[...] = proj(EV_BA, LANES)

    rope = rope_ref[...]
    scale = MLA_QK ** -0.5

    def rot(xx):
        yy = xx * rope
        return yy + pltpu.roll(yy, MLA_ROPE, 1)

    cq = proj(EV_CQ, MLA_Q_RANK)
    cq = cq * lax.rsqrt(jnp.mean(cq * cq, axis=-1, keepdims=True) + NORM_EPS) * qg_ref[...]
    qf = jnp.dot(cq.astype(BF16), wuq_ref[...], preferred_element_type=F32)
    ckv = proj(EV_CKV, MLA_KV_RANK)
    ckv = ckv * lax.rsqrt(jnp.mean(ckv * ckv, axis=-1, keepdims=True) + NORM_EPS) * kvg_ref[...]
    kvf = jnp.dot(ckv.astype(BF16), wukv_ref[...], preferred_element_type=F32)
    kpe = rot(proj(EV_KR, 2 * MLA_ROPE))[:, 0:MLA_ROPE].astype(BF16)
    for hd in range(MLA_HEADS):
        b0 = hd * 2 * LANES
        q_ref[hd, :, 0:MLA_NOPE] = (qf[:, b0:b0 + MLA_NOPE] * scale).astype(BF16)
        qpe = rot(qf[:, b0 + LANES:b0 + 2 * LANES]) * scale
        q_ref[hd, :, MLA_NOPE:MLA_QK] = qpe[:, 0:MLA_ROPE].astype(BF16)
        k_ref[hd, :, 0:MLA_NOPE] = kvf[:, b0:b0 + MLA_NOPE].astype(BF16)
        k_ref[hd, :, MLA_NOPE:MLA_QK] = kpe
        v_ref[hd] = kvf[:, b0 + LANES:b0 + 2 * LANES].astype(BF16)


def _even_in(x, mod3, g, w_in_p, conv_w, rope, qg, wuq_p, kvg, wukv_p):
    s = x.shape[0]
    tm = min(s, 512)
    const = lambda i: (0, 0)
    row = lambda i: (i, 0)
    hrow = lambda i: (0, i, 0)
    return pl.pallas_call(
        _even_in_kernel,
        grid=(s // tm,),
        in_specs=[pl.BlockSpec((tm, D_MODEL), row),
                  pl.BlockSpec((3, D_MODEL), const),
                  pl.BlockSpec((1, D_MODEL), const),
                  pl.BlockSpec((D_MODEL, EV_IN_P), const),
                  pl.BlockSpec((CONV_K, GDN_CONV_CH), const),
                  pl.BlockSpec((tm, LANES), row),
                  pl.BlockSpec((1, MLA_Q_RANK), const),
                  pl.BlockSpec((MLA_Q_RANK, MLA_HEADS * 2 * LANES), const),
                  pl.BlockSpec((1, MLA_KV_RANK), const),
                  pl.BlockSpec((MLA_KV_RANK, MLA_HEADS * 2 * LANES), const)],
        out_specs=[pl.BlockSpec((tm, GDN_CONV_CH), row),
                   pl.BlockSpec((tm, GDN_WIDTH), row),
                   pl.BlockSpec((tm, LANES), row),
                   pl.BlockSpec((MLA_HEADS, tm, MLA_QK), hrow),
                   pl.BlockSpec((MLA_HEADS, tm, MLA_QK), hrow),
                   pl.BlockSpec((MLA_HEADS, tm, MLA_DV), hrow)],
        out_shape=[jax.ShapeDtypeStruct((s, GDN_CONV_CH), F32),
                   jax.ShapeDtypeStruct((s, GDN_WIDTH), F32),
                   jax.ShapeDtypeStruct((s, LANES), F32),
                   jax.ShapeDtypeStruct((MLA_HEADS, s, MLA_QK), BF16),
                   jax.ShapeDtypeStruct((MLA_HEADS, s, MLA_QK), BF16),
                   jax.ShapeDtypeStruct((MLA_HEADS, s, MLA_DV), BF16)],
        scratch_shapes=[pltpu.VMEM((tm + SUBLANES, GDN_WIDTH), F32),
                        pltpu.VMEM((SUBLANES, GDN_CONV_CH), F32)],
        compiler_params=_cparams(("arbitrary",)),
        name="even_in",
    )(x, mod3, g, w_in_p, conv_w, rope, qg, wuq_p, kvg, wukv_p)


GDN_TB = 256


def _gdn_kernel(qkv_ref, ba_ref, z_ref, alog_ref, dtb_ref, gn_ref, o_ref, s_ref):
    tb = GDN_TB
    nch = tb // CHUNK

    @pl.when(pl.program_id(0) == 0)
    def _():
        s_ref[...] = jnp.zeros_like(s_ref)

    ba = ba_ref[...]
    lane = lax.broadcasted_iota(jnp.int32, (tb, LANES), 1)
    beta_all = jax.nn.sigmoid(ba)
    g_all = jnp.where((lane >= GDN_HEADS) & (lane < 2 * GDN_HEADS),
                      -jnp.exp(alog_ref[...]) * _softplus(ba + dtb_ref[...]), 0.0)
    ri = lax.broadcasted_iota(jnp.int32, (tb, tb), 0)
    ci = lax.broadcasted_iota(jnp.int32, (tb, tb), 1)
    same = (ri // CHUNK) == (ci // CHUNK)
    lower = same & (ci <= ri)
    strict = same & (ci < ri)
    eye = (ri == ci).astype(F32)
    gc_all = _mask_dot(lower, g_all)
    gc_t = gc_all.T
    gn = gn_ref[...]

    pre = []
    for hd in range(GDN_HEADS):
        gcol = gc_all[:, GDN_HEADS + hd:GDN_HEADS + hd + 1]
        grow = gc_t[GDN_HEADS + hd:GDN_HEADS + hd + 1, :]
        decay = jnp.where(lower, jnp.exp(jnp.where(lower, gcol - grow, 0.0)), 0.0)
        beta = beta_all[:, hd:hd + 1]
        q = qkv_ref[:, hd * GDN_DK:(hd + 1) * GDN_DK]
        k = qkv_ref[:, GDN_WIDTH + hd * GDN_DK:GDN_WIDTH + (hd + 1) * GDN_DK]
        v = qkv_ref[:, 2 * GDN_WIDTH + hd * GDN_DV:2 * GDN_WIDTH + (hd + 1) * GDN_DV]
        kb = k * beta
        egc = jnp.exp(gcol)
        pw = jnp.where(strict, -(_bdot_nt(kb, k) * decay), 0.0)
        tinv = eye + pw
        for _ in range(5):
            pw = _bdot(pw, pw)
            tinv = tinv + _bdot(tinv, pw)
        sol = _bdot(tinv, jnp.concatenate([v * beta, kb * egc], axis=1))
        u, w = sol[:, 0:GDN_DV], sol[:, GDN_DV:]
        attn = _bdot_nt(q, k) * decay
        qs = q * egc
        gl_row = jnp.where(same & ((ci % CHUNK) == CHUNK - 1), grow, 0.0)
        gl = jnp.sum(gl_row, axis=-1, keepdims=True)
        kend_t = (k * jnp.exp(gl - gcol)).T
        pre.append((u, w, attn, qs, kend_t, jnp.exp(gl)))

    col_chunk = lax.broadcasted_iota(jnp.int32, (GDN_DK, tb), 1) // CHUNK
    for c in range(nch):
        rs = slice(c * CHUNK, (c + 1) * CHUNK)
        for hd in range(GDN_HEADS):
            u, w, attn, qs, kend_t, egl = pre[hd]
            st = s_ref[hd]
            ws = _bdot(jnp.concatenate([w[rs], qs[rs]], axis=0), st)
            v_new = u[rs] - ws[0:CHUNK]
            parts = [jnp.zeros((CHUNK, GDN_DV), F32)] * nch
            parts[c] = v_new
            v_full = jnp.concatenate(parts, axis=0)
            o = ws[CHUNK:] + _bdot(attn[rs], v_full)
            kt = jnp.where(col_chunk == c, kend_t, 0.0)
            s_ref[hd] = st * egl[c * CHUNK:c * CHUNK + 1, :] + _bdot(kt, v_full)
            ms = jnp.mean(o * o, axis=-1, keepdims=True)
            zz = z_ref[rs, hd * GDN_DV:(hd + 1) * GDN_DV]
            o_ref[rs, hd * GDN_DV:(hd + 1) * GDN_DV] = (o * lax.rsqrt(ms + NORM_EPS) * gn * _silu(zz)).astype(BF16)


def _gdn(qkv, ba, z, alog_row, dtb_row, gn):
    s = qkv.shape[0]
    tb = GDN_TB
    const = lambda i: (0, 0)
    row = lambda i: (i, 0)
    return pl.pallas_call(
        _gdn_kernel,
        grid=(s // tb,),
        in_specs=[pl.BlockSpec((tb, GDN_CONV_CH), row),
                  pl.BlockSpec((tb, LANES), row),
                  pl.BlockSpec((tb, GDN_WIDTH), row),
                  pl.BlockSpec((1, LANES), const),
                  pl.BlockSpec((1, LANES), const),
                  pl.BlockSpec((1, GDN_DV), const)],
        out_specs=pl.BlockSpec((tb, GDN_WIDTH), row),
        out_shape=jax.ShapeDtypeStruct((s, GDN_WIDTH), BF16),
        scratch_shapes=[pltpu.VMEM((GDN_HEADS, GDN_DK, GDN_DV), F32)],
        compiler_params=_cparams(("arbitrary",)),
        name="gdn",
    )(qkv, ba, z, alog_row, dtb_row, gn)


def _flash_kernel(q_ref, k_ref, v_ref, o_ref, m_ref, l_ref, acc_ref, *, t):
    qi = pl.program_id(1)
    nrep = t // LANES
    m_ref[...] = jnp.full_like(m_ref, -jnp.inf)
    l_ref[...] = jnp.zeros_like(l_ref)
    acc_ref[...] = jnp.zeros_like(acc_ref)
    q = q_ref[0]

    def attend(kb, diagonal):
        start = pl.multiple_of(kb * t, t)
        k = k_ref[0, pl.ds(start, t), :]
        v = v_ref[0, pl.ds(start, t), :]
        s = lax.dot_general(q, k, (((1,), (1,)), ((), ())), preferred_element_type=F32)
        if diagonal:
            qc = lax.broadcasted_iota(jnp.int32, (t, t), 0) // CHUNK
            kc = lax.broadcasted_iota(jnp.int32, (t, t), 1) // CHUNK
            s = jnp.where(kc <= qc, s, -jnp.inf)
        m_old = m_ref[...]
        m_new = jnp.maximum(m_old, jnp.max(s, axis=-1, keepdims=True))
        alpha = jnp.exp(m_old - m_new)
        p = jnp.exp(s - jnp.concatenate([m_new] * nrep, axis=1))
        psum = p[:, 0:LANES]
        for r in range(1, nrep):
            psum = psum + p[:, r * LANES:(r + 1) * LANES]
        l_ref[...] = l_ref[...] * alpha + psum
        acc_ref[...] = acc_ref[...] * alpha + jnp.dot(p.astype(BF16), v, preferred_element_type=F32)
        m_ref[...] = m_new

    def body(kb, carry):
        attend(kb, False)
        return carry

    lax.fori_loop(0, qi, body, 0)
    attend(qi, True)
    o_ref[...] = (acc_ref[...] / jnp.sum(l_ref[...], axis=-1, keepdims=True)).astype(o_ref.dtype)


def _flash(q, k, v):
    nh, s, _ = q.shape
    t = min(s, 512)
    return pl.pallas_call(
        functools.partial(_flash_kernel, t=t),
        grid=(nh, s // t),
        in_specs=[pl.BlockSpec((1, t, MLA_QK), lambda h, qi: (h, qi, 0)),
                  pl.BlockSpec((1, s, MLA_QK), lambda h, qi: (h, 0, 0)),
                  pl.BlockSpec((1, s, MLA_DV), lambda h, qi: (h, 0, 0))],
        out_specs=pl.BlockSpec((t, MLA_DV), lambda h, qi: (qi, h)),
        out_shape=jax.ShapeDtypeStruct((s, MLA_WIDTH), BF16),
        scratch_shapes=[pltpu.VMEM((t, LANES), F32), pltpu.VMEM((t, LANES), F32), pltpu.VMEM((t, MLA_DV), F32)],
        compiler_params=_cparams(("parallel", "arbitrary")),
        name="mla_flash",
    )(q, k, v)


def _odd_in_kernel(x_ref, mod_ref, g_ref, w_ref, cw_ref, cb_ref, z_ref, xbc_ref, dt_ref, ext_ref, halo_ref, *, tc):
    @pl.when(pl.program_id(0) == 0)
    def _():
        halo_ref[...] = jnp.zeros_like(halo_ref)

    x = x_ref[...]
    h = _adaln(x, g_ref[...], mod_ref[0:1, :], mod_ref[1:2, :]).astype(BF16)
    proj = lambda c0, w: jnp.dot(h, w_ref[:, c0:c0 + w], preferred_element_type=F32)
    for j in range(SSD_D_INNER // tc):
        z_ref[:, j * tc:(j + 1) * tc] = proj(OD_Z + j * tc, tc)
    for j in range(SSD_CONV_CH // tc):
        c0 = j * tc
        y = _causal_conv(proj(OD_XBC + c0, tc), cw_ref[:, c0:c0 + tc], ext_ref, halo_ref, c0)
        xbc_ref[:, c0:c0 + tc] = _silu(y + cb_ref[:, c0:c0 + tc])
    dt_ref[...] = proj(OD_DT, LANES)


def _odd_in(x, mod3, g, w_in_p, conv_w, conv_b):
    s = x.shape[0]
    tm = min(s, 512)
    tc = 512
    const = lambda i: (0, 0)
    row = lambda i: (i, 0)
    return pl.pallas_call(
        functools.partial(_odd_in_kernel, tc=tc),
        grid=(s // tm,),
        in_specs=[pl.BlockSpec((tm, D_MODEL), row),
                  pl.BlockSpec((3, D_MODEL), const),
                  pl.BlockSpec((1, D_MODEL), const),
                  pl.BlockSpec((D_MODEL, OD_IN_P), const),
                  pl.BlockSpec((CONV_K, SSD_CONV_CH), const),
                  pl.BlockSpec((1, SSD_CONV_CH), const)],
        out_specs=[pl.BlockSpec((tm, SSD_D_INNER), row),
                   pl.BlockSpec((tm, SSD_CONV_CH), row),
                   pl.BlockSpec((tm, LANES), row)],
        out_shape=[jax.ShapeDtypeStruct((s, SSD_D_INNER), F32),
                   jax.ShapeDtypeStruct((s, SSD_CONV_CH), F32),
                   jax.ShapeDtypeStruct((s, LANES), F32)],
        scratch_shapes=[pltpu.VMEM((tm + SUBLANES, tc), F32),
                        pltpu.VMEM((SUBLANES, SSD_CONV_CH), F32)],
        compiler_params=_cparams(("arbitrary",)),
        name="odd_in",
    )(x, mod3, g, w_in_p, conv_w, conv_b)


SSD_TB = 256


def _ssd_kernel(xbc_ref, dt_ref, z_ref, alog_ref, dtb_ref, dskip_ref, ng_ref, o_ref, hs_ref):
    tb = SSD_TB

    @pl.when(pl.program_id(0) == 0)
    def _():
        hs_ref[...] = jnp.zeros_like(hs_ref)

    lane = lax.broadcasted_iota(jnp.int32, (tb, LANES), 1)
    dt = jnp.where(lane < SSD_HEADS, _softplus(dt_ref[...] + dtb_ref[...]), 0.0)
    da = dt * (-jnp.exp(alog_ref[...]))
    ri = lax.broadcasted_iota(jnp.int32, (tb, tb), 0)
    ci = lax.broadcasted_iota(jnp.int32, (tb, tb), 1)
    lower = ci <= ri
    acs = _mask_dot(lower, da)
    acs_t = acs.T
    a_last = acs[tb - 1:tb, :]
    d_start = jnp.exp(acs)
    d_end = jnp.exp(a_last - acs)
    c_decay = jnp.exp(a_last)
    left = lane < SSD_HEADDIM

    def pair(arr, h0):
        return jnp.where(left[0:arr.shape[0]], arr[:, h0:h0 + 1], arr[:, h0 + 1:h0 + 2])

    for g in range(SSD_GROUPS):
        xs = xbc_ref[:, g * SSD_GROUP_W:(g + 1) * SSD_GROUP_W]
        bg = xbc_ref[:, SSD_D_INNER + g * SSD_STATE:SSD_D_INNER + (g + 1) * SSD_STATE]
        cg = xbc_ref[:, SSD_D_INNER + (SSD_GROUPS + g) * SSD_STATE:SSD_D_INNER + (SSD_GROUPS + g + 1) * SSD_STATE]
        cb = _bdot_nt(cg, bg)
        hs = hs_ref[g]
        y_off = _bdot(cg, hs)
        y_parts, xde_parts, cdec_parts = [], [], []
        for pr in range(SSD_HEADS_PER_GROUP // 2):
            h0 = g * SSD_HEADS_PER_GROUP + 2 * pr
            xdt = xs[:, pr * LANES:(pr + 1) * LANES] * pair(dt, h0)
            acc = y_off[:, pr * LANES:(pr + 1) * LANES] * pair(d_start, h0)
            for half in range(2):
                col = acs[:, h0 + half:h0 + half + 1]
                rowv = acs_t[h0 + half:h0 + half + 1, :]
                lmat = jnp.where(lower, jnp.exp(jnp.where(lower, col - rowv, 0.0)), 0.0)
                xh = jnp.where(left if half == 0 else jnp.logical_not(left), xdt, 0.0)
                acc = acc + _bdot(cb * lmat, xh)
            y_parts.append(acc)
            xde_parts.append(xdt * pair(d_end, h0))
            cdec_parts.append(pair(c_decay, h0))
        y = jnp.concatenate(y_parts, axis=1)
        hs_ref[g] = hs * jnp.concatenate(cdec_parts, axis=1) + _bdot(bg.T, jnp.concatenate(xde_parts, axis=1))
        y = y + dskip_ref[:, g * SSD_GROUP_W:(g + 1) * SSD_GROUP_W] * xs
        y = y * _silu(z_ref[:, g * SSD_GROUP_W:(g + 1) * SSD_GROUP_W])
        ms = jnp.mean(y * y, axis=-1, keepdims=True)
        y = y * lax.rsqrt(ms + NORM_EPS) * ng_ref[:, g * SSD_GROUP_W:(g + 1) * SSD_GROUP_W]
        o_ref[:, g * SSD_GROUP_W:(g + 1) * SSD_GROUP_W] = y.astype(BF16)


def _ssd(xbc, dt, z, alog_row, dtb_row, dskip_row, ng):
    s = xbc.shape[0]
    tb = SSD_TB
    const = lambda i: (0, 0)
    row = lambda i: (i, 0)
    return pl.pallas_call(
        _ssd_kernel,
        grid=(s // tb,),
        in_specs=[pl.BlockSpec((tb, SSD_CONV_CH), row),
                  pl.BlockSpec((tb, LANES), row),
                  pl.BlockSpec((tb, SSD_D_INNER), row),
                  pl.BlockSpec((1, LANES), const),
                  pl.BlockSpec((1, LANES), const),
                  pl.BlockSpec((1, SSD_D_INNER), const),
                  pl.BlockSpec((1, SSD_D_INNER), const)],
        out_specs=pl.BlockSpec((tb, SSD_D_INNER), row),
        out_shape=jax.ShapeDtypeStruct((s, SSD_D_INNER), BF16),
        scratch_shapes=[pltpu.VMEM((SSD_GROUPS, SSD_STATE, SSD_GROUP_W), F32)],
        compiler_params=_cparams(("arbitrary",)),
        name="ssd",
    )(xbc, dt, z, alog_row, dtb_row, dskip_row, ng)


def _out_kernel(*refs, n_in):
    x_ref, mod_ref = refs[0], refs[1]
    a_refs = refs[2:2 + n_in]
    w_ref = refs[2 + n_in]
    o_ref = refs[3 + n_in]
    y = None
    k0 = 0
    for a_ref in a_refs:
        kw = a_ref.shape[1]
        t = jnp.dot(a_ref[...], w_ref[k0:k0 + kw, :], preferred_element_type=F32)
        y = t if y is None else y + t
        k0 += kw
    o_ref[...] = x_ref[...] + mod_ref[2:3, :] * y


def _out_proj(x, mod3, acts, w):
    s = x.shape[0]
    tm = min(s, 512)
    const = lambda i: (0, 0)
    row = lambda i: (i, 0)
    return pl.pallas_call(
        functools.partial(_out_kernel, n_in=len(acts)),
        grid=(s // tm,),
        in_specs=[pl.BlockSpec((tm, D_MODEL), row), pl.BlockSpec((3, D_MODEL), const)]
                 + [pl.BlockSpec((tm, a.shape[1]), row) for a in acts]
                 + [pl.BlockSpec(w.shape, const)],
        out_specs=pl.BlockSpec((tm, D_MODEL), row),
        out_shape=jax.ShapeDtypeStruct((s, D_MODEL), F32),
        compiler_params=_cparams(("parallel",)),
        name="out_proj",
    )(x, mod3, *acts, w)


def _pad_lanes(v, lane0):
    return jnp.zeros((1, LANES), F32).at[0, lane0:lane0 + v.shape[0]].set(v.astype(F32))


def _swap_halves(w):
    half = w.shape[-1] // 2
    return jnp.concatenate([w[..., half:], w[..., :half]], axis=-1)


def _even_w_in(w):
    qkv, z, beta, a, cq, ckv, kr = jnp.split(
        w, [GDN_CONV_CH, GDN_CONV_CH + GDN_WIDTH, GDN_CONV_CH + GDN_WIDTH + GDN_HEADS,
            GDN_CONV_CH + GDN_WIDTH + 2 * GDN_HEADS, GDN_CONV_CH + GDN_WIDTH + 2 * GDN_HEADS + MLA_Q_RANK,
            GDN_CONV_CH + GDN_WIDTH + 2 * GDN_HEADS + MLA_Q_RANK + MLA_KV_RANK], axis=1)
    pad = jnp.zeros((D_MODEL, LANES - 2 * GDN_HEADS), w.dtype)
    return jnp.concatenate([qkv, z, cq, ckv, kr, _swap_halves(kr), beta, a, pad], axis=1).astype(BF16)


def _even_w_uq(w):
    pe = w[..., MLA_NOPE:]
    return jnp.concatenate([w, _swap_halves(pe)], axis=-1).reshape(MLA_Q_RANK, MLA_HEADS * 2 * LANES).astype(BF16)


def _odd_w_in(w):
    pad = jnp.zeros((D_MODEL, LANES - SSD_HEADS), w.dtype)
    return jnp.concatenate([w, pad], axis=1).astype(BF16)


def kernel(x, c, positions, ada_w, ada_b, norm_g, ffn_w1, ffn_w3, ffn_w2, ev_w_in, gdn_conv_w, gdn_A_log, gdn_dt_bias, gdn_norm_g, mla_q_norm_g, mla_w_uq, mla_kv_norm_g, mla_w_ukv, ev_w_out, ssd_w_in, ssd_conv_w, ssd_conv_b, ssd_A_log, ssd_dt_bias, ssd_D, ssd_norm_g, ssd_w_out, final_g):
    b, s, d = x.shape
    assert b == 1 and d == D_MODEL and s % 256 == 0
    xs = x.reshape(s, d)
    mod = _modulation(c.reshape(d, 1), ada_w, ada_b).reshape(DEPTH, 3, 3, d)
    rope = _rope_table(positions.reshape(s, 1))
    fg = final_g.reshape(1, d)
    for l in range(DEPTH):
        last = l == DEPTH - 1
        xs = _ffn(xs, mod[l, 0], norm_g[l, 0].reshape(1, d), ffn_w1[l, 0].astype(BF16), ffn_w3[l, 0].astype(BF16),
                  ffn_w2[l, 0].astype(BF16), fg, False)
        g1 = norm_g[l, 1].reshape(1, d)
        if l % 2 == 0:
            e = l // 2
            qkv, z, ba, q, k, v = _even_in(
                xs, mod[l, 1], g1, _even_w_in(ev_w_in[e]), gdn_conv_w[e], rope,
                mla_q_norm_g[e].reshape(1, -1), _even_w_uq(mla_w_uq[e]),
                mla_kv_norm_g[e].reshape(1, -1),
                mla_w_ukv[e].reshape(MLA_KV_RANK, MLA_HEADS * 2 * LANES).astype(BF16))
            o_a = _gdn(qkv, ba, z, _pad_lanes(gdn_A_log[e], GDN_HEADS), _pad_lanes(gdn_dt_bias[e], GDN_HEADS),
                       gdn_norm_g[e].reshape(1, -1))
            o_b = _flash(q, k, v)
            xs = _out_proj(xs, mod[l, 1], [o_a, o_b], ev_w_out[e].astype(BF16))
        else:
            o = l // 2
            z, xbc, dt = _odd_in(xs, mod[l, 1], g1, _odd_w_in(ssd_w_in[o]), ssd_conv_w[o],
                                 ssd_conv_b[o].reshape(1, -1))
            y = _ssd(xbc, dt, z, _pad_lanes(ssd_A_log[o], 0), _pad_lanes(ssd_dt_bias[o], 0),
                     jnp.repeat(ssd_D[o].astype(F32), SSD_HEADDIM).reshape(1, -1), ssd_norm_g[o].reshape(1, -1))
            xs = _out_proj(xs, mod[l, 1], [y], ssd_w_out[o].astype(BF16))
        xs = _ffn(xs, mod[l, 2], norm_g[l, 2].reshape(1, d), ffn_w1[l, 1].astype(BF16), ffn_w3[l, 1].astype(BF16),
                  ffn_w2[l, 1].astype(BF16), fg, last)
    return xs.reshape(b, s, d)
```

```python
import functools
import math

import jax
import jax.numpy as jnp
from jax import lax
from jax.experimental import pallas as pl
from jax.experimental.pallas import tpu as pltpu

F32 = jnp.float32
BF16 = jnp.bfloat16

D_MODEL = 1024
DEPTH = 4
CHUNK = 64
NORM_EPS = 1e-6
CONV_K = 4
D_FF = 2816
N_MOD = 9

GDN_HEADS = 4
GDN_DK = 128
GDN_DV = 128
GDN_WIDTH = GDN_HEADS * GDN_DV
GDN_CONV_CH = 2 * GDN_HEADS * GDN_DK + GDN_WIDTH

MLA_HEADS = 4
MLA_Q_RANK = 384
MLA_KV_RANK = 256
MLA_NOPE = 128
MLA_ROPE = 64
MLA_DV = 128
MLA_WIDTH = MLA_HEADS * MLA_DV
MLA_QK = MLA_NOPE + MLA_ROPE
ROPE_THETA = 10000.0

SSD_D_INNER = 2 * D_MODEL
SSD_HEADDIM = 64
SSD_HEADS = SSD_D_INNER // SSD_HEADDIM
SSD_GROUPS = 4
SSD_STATE = 128
SSD_CONV_CH = SSD_D_INNER + 2 * SSD_GROUPS * SSD_STATE
SSD_GROUP_W = SSD_D_INNER // SSD_GROUPS
SSD_HEADS_PER_GROUP = SSD_HEADS // SSD_GROUPS

LANES = 128
SUBLANES = 8
VMEM_LIMIT = 56 * 1024 * 1024

EV_QKV = 0
EV_Z = GDN_CONV_CH
EV_CQ = EV_Z + GDN_WIDTH
EV_CKV = EV_CQ + MLA_Q_RANK
EV_KR = EV_CKV + MLA_KV_RANK
EV_BA = EV_KR + 2 * MLA_ROPE
EV_IN_P = EV_BA + LANES

OD_Z = 0
OD_XBC = SSD_D_INNER
OD_DT = OD_XBC + SSD_CONV_CH
OD_IN_P = OD_DT + LANES


def _cparams(sem):
    return pltpu.CompilerParams(dimension_semantics=sem, vmem_limit_bytes=VMEM_LIMIT)


def _bdot(a, b):
    return jnp.dot(a.astype(BF16), b.astype(BF16), preferred_element_type=F32)


def _bdot_nt(a, b):
    return lax.dot_general(a.astype(BF16), b.astype(BF16), (((1,), (1,)), ((), ())),
                           preferred_element_type=F32)


def _mask_dot(mask, x):
    m = jnp.where(mask, 1.0, 0.0).astype(BF16)
    out = None
    r = x
    for _ in range(3):
        part = r.astype(BF16)
        r = r - part.astype(F32)
        t = jnp.dot(m, part, preferred_element_type=F32)
        out = t if out is None else out + t
    return out


def _dot3(a, b):
    ah = a.astype(BF16)
    al = (a - ah.astype(F32)).astype(BF16)
    bh = b.astype(BF16)
    bl = (b - bh.astype(F32)).astype(BF16)
    d = lambda p, q: jnp.dot(p, q, preferred_element_type=F32)
    return d(ah, bh) + (d(ah, bl) + d(al, bh))


def _silu(x):
    return x * jax.nn.sigmoid(x)


def _softplus(x):
    return jnp.maximum(x, 0.0) + jnp.log1p(jnp.exp(-jnp.abs(x)))


def _adaln(x, g, shift, scale):
    ms = jnp.mean(x * x, axis=-1, keepdims=True)
    y = x * lax.rsqrt(ms + NORM_EPS) * g
    return y * (1.0 + scale) + shift


def _mod_kernel(c_ref, w_ref, b_ref, o_ref):
    ca = _silu(c_ref[...])
    cb = jnp.broadcast_to(ca, (D_MODEL, LANES))
    tn = w_ref.shape[2]
    for t in range(tn // LANES):
        sl = slice(t * LANES, (t + 1) * LANES)
        o_ref[0, :, sl] = jnp.sum(w_ref[0, :, sl] * cb, axis=0, keepdims=True) + b_ref[0, :, sl]


def _modulation(c_col, ada_w, ada_b):
    n = N_MOD * D_MODEL
    tn = n // 4
    return pl.pallas_call(
        _mod_kernel,
        grid=(DEPTH, n // tn),
        in_specs=[pl.BlockSpec((D_MODEL, 1), lambda l, j: (0, 0)),
                  pl.BlockSpec((1, D_MODEL, tn), lambda l, j: (l, 0, j)),
                  pl.BlockSpec((1, 1, tn), lambda l, j: (l, 0, j))],
        out_specs=pl.BlockSpec((1, 1, tn), lambda l, j: (l, 0, j)),
        out_shape=jax.ShapeDtypeStruct((DEPTH, 1, n), F32),
        compiler_params=_cparams(("parallel", "parallel")),
        name="modulation",
    )(c_col, ada_w, ada_b.reshape(DEPTH, 1, n))


def _rope_kernel(pos_ref, freq_ref, o_ref):
    ang = pos_ref[...].astype(F32) * freq_ref[...]
    lane = lax.broadcasted_iota(jnp.int32, ang.shape, 1)
    sgn = jnp.where(lane < 3 * (MLA_ROPE // 2), -1.0, 1.0)
    o_ref[...] = jnp.where(lane < MLA_ROPE, jnp.cos(ang), sgn * jnp.sin(ang))


def _rope_table(pos_col):
    s = pos_col.shape[0]
    tm = min(s, 2048)
    half = MLA_ROPE // 2
    inv_freq = ROPE_THETA ** (-jnp.arange(half, dtype=F32) / half)
    freq = jnp.tile(inv_freq, LANES // half).reshape(1, LANES)
    return pl.pallas_call(
        _rope_kernel,
        grid=(s // tm,),
        in_specs=[pl.BlockSpec((tm, 1), lambda i: (i, 0)),
                  pl.BlockSpec((1, LANES), lambda i: (0, 0))],
        out_specs=pl.BlockSpec((tm, LANES), lambda i: (i, 0)),
        out_shape=jax.ShapeDtypeStruct((s, LANES), F32),
        compiler_params=_cparams(("parallel",)),
        name="rope_table",
    )(pos_col, freq)


def _ffn_kernel(x_ref, mod_ref, g_ref, w1_ref, w3_ref, w2_ref, fg_ref, o_ref, a_ref, *, tf, final):
    x = x_ref[...]
    h = _adaln(x, g_ref[...], mod_ref[0:1, :], mod_ref[1:2, :]).astype(BF16)
    for j in range(D_FF // tf):
        sl = slice(j * tf, (j + 1) * tf)
        h1 = jnp.dot(h, w1_ref[:, sl], preferred_element_type=F32)
        h3 = jnp.dot(h, w3_ref[:, sl], preferred_element_type=F32)
        a_ref[:, sl] = (_silu(h1) * h3).astype(BF16)
    y = jnp.dot(a_ref[...], w2_ref[...], preferred_element_type=F32)
    xn = x + (0.5 * mod_ref[2:3, :]) * y
    if final:
        ms = jnp.mean(xn * xn, axis=-1, keepdims=True)
        xn = xn * lax.rsqrt(ms + NORM_EPS) * fg_ref[...]
    o_ref[...] = xn


def _ffn(x, mod3, g, w1, w3, w2, final_g, final):
    s = x.shape[0]
    tm = min(s, 512)
    const = lambda i: (0, 0)
    return pl.pallas_call(
        functools.partial(_ffn_kernel, tf=256, final=final),
        grid=(s // tm,),
        in_specs=[pl.BlockSpec((tm, D_MODEL), lambda i: (i, 0)),
                  pl.BlockSpec((3, D_MODEL), const),
                  pl.BlockSpec((1, D_MODEL), const),
                  pl.BlockSpec((D_MODEL, D_FF), const, pipeline_mode=pl.Buffered(1)),
                  pl.BlockSpec((D_MODEL, D_FF), const, pipeline_mode=pl.Buffered(1)),
                  pl.BlockSpec((D_FF, D_MODEL), const, pipeline_mode=pl.Buffered(1)),
                  pl.BlockSpec((1, D_MODEL), const)],
        out_specs=pl.BlockSpec((tm, D_MODEL), lambda i: (i, 0)),
        out_shape=jax.ShapeDtypeStruct((s, D_MODEL), F32),
        scratch_shapes=[pltpu.VMEM((tm, D_FF), BF16)],
        compiler_params=_cparams(("parallel",)),
        name="ffn_final" if final else "ffn",
    )(x, mod3, g, w1, w3, w2, final_g)


def _causal_conv(p, cw, ext_ref, halo_ref, col0):
    tm, w = p.shape
    cs = slice(col0, col0 + w)
    ext_ref[0:SUBLANES, 0:w] = halo_ref[:, cs]
    ext_ref[SUBLANES:SUBLANES + tm, 0:w] = p
    halo_ref[:, cs] = p[tm - SUBLANES:, :]
    y = p * cw[CONV_K - 1:CONV_K, :]
    for j in range(CONV_K - 1):
        off = SUBLANES - (CONV_K - 1) + j
        y = y + ext_ref[off:off + tm, 0:w] * cw[j:j + 1, :]
    return y


def _even_in_kernel(x_ref, mod_ref, g_ref, w_ref, cw_ref, rope_ref, qg_ref, wuq_ref, kvg_ref, wukv_ref,
                    qkv_ref, z_ref, ba_ref, q_ref, k_ref, v_ref, ext_ref, halo_ref):
    @pl.when(pl.program_id(0) == 0)
    def _():
        halo_ref[...] = jnp.zeros_like(halo_ref)

    x = x_ref[...]
    h = _adaln(x, g_ref[...], mod_ref[0:1, :], mod_ref[1:2, :]).astype(BF16)
    proj = lambda c0, w: jnp.dot(h, w_ref[:, c0:c0 + w], preferred_element_type=F32)

    for part in range(3):
        c0 = part * GDN_WIDTH
        y = _silu(_causal_conv(proj(c0, GDN_WIDTH), cw_ref[:, c0:c0 + GDN_WIDTH], ext_ref, halo_ref, c0))
        if part == 2:
            qkv_ref[:, c0:c0 + GDN_WIDTH] = y
        else:
            post = GDN_DK ** -0.5 if part == 0 else 1.0
            for hd in range(GDN_HEADS):
                yh = y[:, hd * GDN_DK:(hd + 1) * GDN_DK]
                nrm = lax.rsqrt(jnp.sum(yh * yh, axis=-1, keepdims=True) + NORM_EPS)
                qkv_ref[:, c0 + hd * GDN_DK:c0 + (hd + 1) * GDN_DK] = yh * nrm * post
    z_ref[...] = proj(EV_Z, GDN_WIDTH)
    ba_ref[...] = proj(EV_BA, LANES)

    rope = rope_ref[...]
    scale = MLA_QK ** -0.5 * math.log2(math.e)

    def rot(xx):
        yy = xx * rope
        return yy + pltpu.roll(yy, MLA_ROPE, 1)

    cq = proj(EV_CQ, MLA_Q_RANK)
    cq = cq * lax.rsqrt(jnp.mean(cq * cq, axis=-1, keepdims=True) + NORM_EPS) * qg_ref[...]
    qf = jnp.dot(cq.astype(BF16), wuq_ref[...], preferred_element_type=F32)
    ckv = proj(EV_CKV, MLA_KV_RANK)
    ckv = ckv * lax.rsqrt(jnp.mean(ckv * ckv, axis=-1, keepdims=True) + NORM_EPS) * kvg_ref[...]
    kvf = jnp.dot(ckv.astype(BF16), wukv_ref[...], preferred_element_type=F32)
    kpe = rot(proj(EV_KR, 2 * MLA_ROPE))[:, 0:MLA_ROPE].astype(BF16)
    for hd in range(MLA_HEADS):
        b0 = hd * 2 * LANES
        q_ref[hd, :, 0:MLA_NOPE] = (qf[:, b0:b0 + MLA_NOPE] * scale).astype(BF16)
        qpe = rot(qf[:, b0 + LANES:b0 + 2 * LANES]) * scale
        q_ref[hd, :, MLA_NOPE:MLA_QK] = qpe[:, 0:MLA_ROPE].astype(BF16)
        k_ref[hd, :, 0:MLA_NOPE] = kvf[:, b0:b0 + MLA_NOPE].astype(BF16)
        k_ref[hd, :, MLA_NOPE:MLA_QK] = kpe
        v_ref[hd] = kvf[:, b0 + LANES:b0 + 2 * LANES].astype(BF16)


def _even_in(x, mod3, g, w_in_p, conv_w, rope, qg, wuq_p, kvg, wukv_p):
    s = x.shape[0]
    tm = min(s, 512)
    const = lambda i: (0, 0)
    row = lambda i: (i, 0)
    hrow = lambda i: (0, i, 0)
    return pl.pallas_call(
        _even_in_kernel,
        grid=(s // tm,),
        in_specs=[pl.BlockSpec((tm, D_MODEL), row),
                  pl.BlockSpec((3, D_MODEL), const),
                  pl.BlockSpec((1, D_MODEL), const),
                  pl.BlockSpec((D_MODEL, EV_IN_P), const),
                  pl.BlockSpec((CONV_K, GDN_CONV_CH), const),
                  pl.BlockSpec((tm, LANES), row),
                  pl.BlockSpec((1, MLA_Q_RANK), const),
                  pl.BlockSpec((MLA_Q_RANK, MLA_HEADS * 2 * LANES), const),
                  pl.BlockSpec((1, MLA_KV_RANK), const),
                  pl.BlockSpec((MLA_KV_RANK, MLA_HEADS * 2 * LANES), const)],
        out_specs=[pl.BlockSpec((tm, GDN_CONV_CH), row),
                   pl.BlockSpec((tm, GDN_WIDTH), row),
                   pl.BlockSpec((tm, LANES), row),
                   pl.BlockSpec((MLA_HEADS, tm, MLA_QK), hrow),
                   pl.BlockSpec((MLA_HEADS, tm, MLA_QK), hrow),
                   pl.BlockSpec((MLA_HEADS, tm, MLA_DV), hrow)],
        out_shape=[jax.ShapeDtypeStruct((s, GDN_CONV_CH), F32),
                   jax.ShapeDtypeStruct((s, GDN_WIDTH), F32),
                   jax.ShapeDtypeStruct((s, LANES), F32),
                   jax.ShapeDtypeStruct((MLA_HEADS, s, MLA_QK), BF16),
                   jax.ShapeDtypeStruct((MLA_HEADS, s, MLA_QK), BF16),
                   jax.ShapeDtypeStruct((MLA_HEADS, s, MLA_DV), BF16)],
        scratch_shapes=[pltpu.VMEM((tm + SUBLANES, GDN_WIDTH), F32),
                        pltpu.VMEM((SUBLANES, GDN_CONV_CH), F32)],
        compiler_params=_cparams(("arbitrary",)),
        name="even_in",
    )(x, mod3, g, w_in_p, conv_w, rope, qg, wuq_p, kvg, wukv_p)


GDN_TB = 256


def _gdn_kernel(qkv_ref, ba_ref, z_ref, alog_ref, dtb_ref, gn_ref, o_ref, s_ref):
    tb = GDN_TB
    nch = tb // CHUNK

    @pl.when(pl.program_id(0) == 0)
    def _():
        s_ref[...] = jnp.zeros_like(s_ref)

    ba = ba_ref[...]
    lane = lax.broadcasted_iota(jnp.int32, (tb, LANES), 1)
    beta_all = jax.nn.sigmoid(ba)
    g_all = jnp.where((lane >= GDN_HEADS) & (lane < 2 * GDN_HEADS),
                      -jnp.exp(alog_ref[...]) * _softplus(ba + dtb_ref[...]), 0.0)
    ri = lax.broadcasted_iota(jnp.int32, (tb, tb), 0)
    ci = lax.broadcasted_iota(jnp.int32, (tb, tb), 1)
    same = (ri // CHUNK) == (ci // CHUNK)
    lower = same & (ci <= ri)
    strict = same & (ci < ri)
    eye = (ri == ci).astype(F32)
    gc_all = _mask_dot(lower, g_all)
    gc_t = gc_all.T
    gn = gn_ref[...]

    pre = []
    for hd in range(GDN_HEADS):
        gcol = gc_all[:, GDN_HEADS + hd:GDN_HEADS + hd + 1]
        grow = gc_t[GDN_HEADS + hd:GDN_HEADS + hd + 1, :]
        decay = jnp.where(lower, jnp.exp(jnp.where(lower, gcol - grow, 0.0)), 0.0)
        beta = beta_all[:, hd:hd + 1]
        q = qkv_ref[:, hd * GDN_DK:(hd + 1) * GDN_DK]
        k = qkv_ref[:, GDN_WIDTH + hd * GDN_DK:GDN_WIDTH + (hd + 1) * GDN_DK]
        v = qkv_ref[:, 2 * GDN_WIDTH + hd * GDN_DV:2 * GDN_WIDTH + (hd + 1) * GDN_DV]
        kb = k * beta
        egc = jnp.exp(gcol)
        pw = jnp.where(strict, -(_bdot_nt(kb, k) * decay), 0.0)
        tinv = eye + pw
        for _ in range(5):
            pw = _dot3(pw, pw)
            tinv = tinv + _dot3(tinv, pw)
        sol = _bdot(tinv, jnp.concatenate([v * beta, kb * egc], axis=1))
        u, w = sol[:, 0:GDN_DV], sol[:, GDN_DV:]
        attn = _bdot_nt(q, k) * decay
        qs = q * egc
        gl_row = jnp.where(same & ((ci % CHUNK) == CHUNK - 1), grow, 0.0)
        gl = jnp.sum(gl_row, axis=-1, keepdims=True)
        kend_t = (k * jnp.exp(gl - gcol)).T
        pre.append((u, w, attn, qs, kend_t, jnp.exp(gl)))

    col_chunk = lax.broadcasted_iota(jnp.int32, (GDN_DK, tb), 1) // CHUNK
    for c in range(nch):
        rs = slice(c * CHUNK, (c + 1) * CHUNK)
        for hd in range(GDN_HEADS):
            u, w, attn, qs, kend_t, egl = pre[hd]
            st = s_ref[hd]
            ws = _bdot(jnp.concatenate([w[rs], qs[rs]], axis=0), st)
            v_new = u[rs] - ws[0:CHUNK]
            parts = [jnp.zeros((CHUNK, GDN_DV), F32)] * nch
            parts[c] = v_new
            v_full = jnp.concatenate(parts, axis=0)
            o = ws[CHUNK:] + _bdot(attn[rs], v_full)
            kt = jnp.where(col_chunk == c, kend_t, 0.0)
            s_ref[hd] = st * egl[c * CHUNK:c * CHUNK + 1, :] + _bdot(kt, v_full)
            ms = jnp.mean(o * o, axis=-1, keepdims=True)
            zz = z_ref[rs, hd * GDN_DV:(hd + 1) * GDN_DV]
            o_ref[rs, hd * GDN_DV:(hd + 1) * GDN_DV] = (o * lax.rsqrt(ms + NORM_EPS) * gn * _silu(zz)).astype(BF16)


def _gdn(qkv, ba, z, alog_row, dtb_row, gn):
    s = qkv.shape[0]
    tb = GDN_TB
    const = lambda i: (0, 0)
    row = lambda i: (i, 0)
    return pl.pallas_call(
        _gdn_kernel,
        grid=(s // tb,),
        in_specs=[pl.BlockSpec((tb, GDN_CONV_CH), row),
                  pl.BlockSpec((tb, LANES), row),
                  pl.BlockSpec((tb, GDN_WIDTH), row),
                  pl.BlockSpec((1, LANES), const),
                  pl.BlockSpec((1, LANES), const),
                  pl.BlockSpec((1, GDN_DV), const)],
        out_specs=pl.BlockSpec((tb, GDN_WIDTH), row),
        out_shape=jax.ShapeDtypeStruct((s, GDN_WIDTH), BF16),
        scratch_shapes=[pltpu.VMEM((GDN_HEADS, GDN_DK, GDN_DV), F32)],
        compiler_params=_cparams(("arbitrary",)),
        name="gdn",
    )(qkv, ba, z, alog_row, dtb_row, gn)


def _flash_kernel(q_ref, k_ref, v_ref, o_ref, m_ref, l_ref, acc_ref, *, t):
    qi = pl.program_id(1)
    nrep = t // LANES
    m_ref[...] = jnp.full_like(m_ref, -jnp.inf)
    l_ref[...] = jnp.zeros_like(l_ref)
    acc_ref[...] = jnp.zeros_like(acc_ref)

    def attend(kb, half, diagonal):
        start = pl.multiple_of(kb * t, t)
        k = k_ref[0, pl.ds(start, t), :]
        v = v_ref[0, pl.ds(start, t), :]
        q = q_ref[0, half * t:(half + 1) * t, :]
        s = lax.dot_general(q, k, (((1,), (1,)), ((), ())), preferred_element_type=F32)
        if diagonal:
            qc = lax.broadcasted_iota(jnp.int32, (t, t), 0) // CHUNK
            kc = lax.broadcasted_iota(jnp.int32, (t, t), 1) // CHUNK
            s = jnp.where(kc <= qc, s, -jnp.inf)
        m_old = m_ref[half]
        m_new = jnp.maximum(m_old, jnp.max(s, axis=-1, keepdims=True))
        alpha = jnp.exp2(m_old - m_new)
        p = jnp.exp2(s - jnp.concatenate([m_new] * nrep, axis=1))
        psum = p[:, 0:LANES]
        for r in range(1, nrep):
            psum = psum + p[:, r * LANES:(r + 1) * LANES]
        l_ref[half] = l_ref[half] * alpha + psum
        acc_ref[half] = acc_ref[half] * alpha + jnp.dot(p.astype(BF16), v, preferred_element_type=F32)
        m_ref[half] = m_new

    def body(i, carry):
        for kb in (2 * i, 2 * i + 1):
            attend(kb, 0, False)
            attend(kb, 1, False)
        return carry

    lax.fori_loop(0, qi, body, 0)
    attend(2 * qi, 0, True)
    attend(2 * qi, 1, False)
    attend(2 * qi + 1, 1, True)
    for half in range(2):
        l = jnp.sum(l_ref[half], axis=-1, keepdims=True)
        o_ref[half * t:(half + 1) * t, :] = (acc_ref[half] / l).astype(o_ref.dtype)


def _flash(q, k, v):
    nh, s, _ = q.shape
    t = 512
    assert s % (2 * t) == 0
    return pl.pallas_call(
        functools.partial(_flash_kernel, t=t),
        grid=(nh, s // (2 * t)),
        in_specs=[pl.BlockSpec((1, 2 * t, MLA_QK), lambda h, qi: (h, qi, 0)),
                  pl.BlockSpec((1, s, MLA_QK), lambda h, qi: (h, 0, 0)),
                  pl.BlockSpec((1, s, MLA_DV), lambda h, qi: (h, 0, 0))],
        out_specs=pl.BlockSpec((2 * t, MLA_DV), lambda h, qi: (qi, h)),
        out_shape=jax.ShapeDtypeStruct((s, MLA_WIDTH), BF16),
        scratch_shapes=[pltpu.VMEM((2, t, LANES), F32), pltpu.VMEM((2, t, LANES), F32),
                        pltpu.VMEM((2, t, MLA_DV), F32)],
        compiler_params=_cparams(("parallel", "arbitrary")),
        name="mla_flash",
    )(q, k, v)


def _odd_in_kernel(x_ref, mod_ref, g_ref, w_ref, cw_ref, cb_ref, z_ref, xbc_ref, dt_ref, ext_ref, halo_ref, *, tc):
    @pl.when(pl.program_id(0) == 0)
    def _():
        halo_ref[...] = jnp.zeros_like(halo_ref)

    x = x_ref[...]
    h = _adaln(x, g_ref[...], mod_ref[0:1, :], mod_ref[1:2, :]).astype(BF16)
    proj = lambda c0, w: jnp.dot(h, w_ref[:, c0:c0 + w], preferred_element_type=F32)
    for j in range(SSD_D_INNER // tc):
        z_ref[:, j * tc:(j + 1) * tc] = proj(OD_Z + j * tc, tc)
    for j in range(SSD_CONV_CH // tc):
        c0 = j * tc
        y = _causal_conv(proj(OD_XBC + c0, tc), cw_ref[:, c0:c0 + tc], ext_ref, halo_ref, c0)
        xbc_ref[:, c0:c0 + tc] = _silu(y + cb_ref[:, c0:c0 + tc])
    dt_ref[...] = proj(OD_DT, LANES)


def _odd_in(x, mod3, g, w_in_p, conv_w, conv_b):
    s = x.shape[0]
    tm = min(s, 512)
    tc = 512
    const = lambda i: (0, 0)
    row = lambda i: (i, 0)
    return pl.pallas_call(
        functools.partial(_odd_in_kernel, tc=tc),
        grid=(s // tm,),
        in_specs=[pl.BlockSpec((tm, D_MODEL), row),
                  pl.BlockSpec((3, D_MODEL), const),
                  pl.BlockSpec((1, D_MODEL), const),
                  pl.BlockSpec((D_MODEL, OD_IN_P), const),
                  pl.BlockSpec((CONV_K, SSD_CONV_CH), const),
                  pl.BlockSpec((1, SSD_CONV_CH), const)],
        out_specs=[pl.BlockSpec((tm, SSD_D_INNER), row),
                   pl.BlockSpec((tm, SSD_CONV_CH), row),
                   pl.BlockSpec((tm, LANES), row)],
        out_shape=[jax.ShapeDtypeStruct((s, SSD_D_INNER), F32),
                   jax.ShapeDtypeStruct((s, SSD_CONV_CH), F32),
                   jax.ShapeDtypeStruct((s, LANES), F32)],
        scratch_shapes=[pltpu.VMEM((tm + SUBLANES, tc), F32),
                        pltpu.VMEM((SUBLANES, SSD_CONV_CH), F32)],
        compiler_params=_cparams(("arbitrary",)),
        name="odd_in",
    )(x, mod3, g, w_in_p, conv_w, conv_b)


SSD_TB = 256


def _ssd_kernel(xbc_ref, dt_ref, z_ref, alog_ref, dtb_ref, dskip_ref, ng_ref, o_ref, hs_ref):
    tb = SSD_TB

    @pl.when(pl.program_id(0) == 0)
    def _():
        hs_ref[...] = jnp.zeros_like(hs_ref)

    lane = lax.broadcasted_iota(jnp.int32, (tb, LANES), 1)
    dt = jnp.where(lane < SSD_HEADS, _softplus(dt_ref[...] + dtb_ref[...]), 0.0)
    da = dt * (-jnp.exp(alog_ref[...]))
    ri = lax.broadcasted_iota(jnp.int32, (tb, tb), 0)
    ci = lax.broadcasted_iota(jnp.int32, (tb, tb), 1)
    lower = ci <= ri
    acs = _mask_dot(lower, da)
    acs_t = acs.T
    a_last = acs[tb - 1:tb, :]
    d_start = jnp.exp(acs)
    d_end = jnp.exp(a_last - acs)
    c_decay = jnp.exp(a_last)
    left = lane < SSD_HEADDIM

    def pair(arr, h0):
        return jnp.where(left[0:arr.shape[0]], arr[:, h0:h0 + 1], arr[:, h0 + 1:h0 + 2])

    for g in range(SSD_GROUPS):
        xs = xbc_ref[:, g * SSD_GROUP_W:(g + 1) * SSD_GROUP_W]
        bg = xbc_ref[:, SSD_D_INNER + g * SSD_STATE:SSD_D_INNER + (g + 1) * SSD_STATE]
        cg = xbc_ref[:, SSD_D_INNER + (SSD_GROUPS + g) * SSD_STATE:SSD_D_INNER + (SSD_GROUPS + g + 1) * SSD_STATE]
        cb = _bdot_nt(cg, bg)
        hs = hs_ref[g]
        y_off = _bdot(cg, hs)
        y_parts, xde_parts, cdec_parts = [], [], []
        for pr in range(SSD_HEADS_PER_GROUP // 2):
            h0 = g * SSD_HEADS_PER_GROUP + 2 * pr
            xdt = xs[:, pr * LANES:(pr + 1) * LANES] * pair(dt, h0)
            acc = y_off[:, pr * LANES:(pr + 1) * LANES] * pair(d_start, h0)
            for half in range(2):
                col = acs[:, h0 + half:h0 + half + 1]
                rowv = acs_t[h0 + half:h0 + half + 1, :]
                lmat = jnp.where(lower, jnp.exp(jnp.where(lower, col - rowv, 0.0)), 0.0)
                xh = jnp.where(left if half == 0 else jnp.logical_not(left), xdt, 0.0)
                acc = acc + _bdot(cb * lmat, xh)
            y_parts.append(acc)
            xde_parts.append(xdt * pair(d_end, h0))
            cdec_parts.append(pair(c_decay, h0))
        y = jnp.concatenate(y_parts, axis=1)
        hs_ref[g] = hs * jnp.concatenate(cdec_parts, axis=1) + _bdot(bg.T, jnp.concatenate(xde_parts, axis=1))
        y = y + dskip_ref[:, g * SSD_GROUP_W:(g + 1) * SSD_GROUP_W] * xs
        y = y * _silu(z_ref[:, g * SSD_GROUP_W:(g + 1) * SSD_GROUP_W])
        ms = jnp.mean(y * y, axis=-1, keepdims=True)
        y = y * lax.rsqrt(ms + NORM_EPS) * ng_ref[:, g * SSD_GROUP_W:(g + 1) * SSD_GROUP_W]
        o_ref[:, g * SSD_GROUP_W:(g + 1) * SSD_GROUP_W] = y.astype(BF16)


def _ssd(xbc, dt, z, alog_row, dtb_row, dskip_row, ng):
    s = xbc.shape[0]
    tb = SSD_TB
    const = lambda i: (0, 0)
    row = lambda i: (i, 0)
    return pl.pallas_call(
        _ssd_kernel,
        grid=(s // tb,),
        in_specs=[pl.BlockSpec((tb, SSD_CONV_CH), row),
                  pl.BlockSpec((tb, LANES), row),
                  pl.BlockSpec((tb, SSD_D_INNER), row),
                  pl.BlockSpec((1, LANES), const),
                  pl.BlockSpec((1, LANES), const),
                  pl.BlockSpec((1, SSD_D_INNER), const),
                  pl.BlockSpec((1, SSD_D_INNER), const)],
        out_specs=pl.BlockSpec((tb, SSD_D_INNER), row),
        out_shape=jax.ShapeDtypeStruct((s, SSD_D_INNER), BF16),
        scratch_shapes=[pltpu.VMEM((SSD_GROUPS, SSD_STATE, SSD_GROUP_W), F32)],
        compiler_params=_cparams(("arbitrary",)),
        name="ssd",
    )(xbc, dt, z, alog_row, dtb_row, dskip_row, ng)


def _out_kernel(*refs, n_in):
    x_ref, mod_ref = refs[0], refs[1]
    a_refs = refs[2:2 + n_in]
    w_ref = refs[2 + n_in]
    o_ref = refs[3 + n_in]
    y = None
    k0 = 0
    for a_ref in a_refs:
        kw = a_ref.shape[1]
        t = jnp.dot(a_ref[...], w_ref[k0:k0 + kw, :], preferred_element_type=F32)
        y = t if y is None else y + t
        k0 += kw
    o_ref[...] = x_ref[...] + mod_ref[2:3, :] * y


def _out_proj(x, mod3, acts, w):
    s = x.shape[0]
    tm = min(s, 512)
    const = lambda i: (0, 0)
    row = lambda i: (i, 0)
    return pl.pallas_call(
        functools.partial(_out_kernel, n_in=len(acts)),
        grid=(s // tm,),
        in_specs=[pl.BlockSpec((tm, D_MODEL), row), pl.BlockSpec((3, D_MODEL), const)]
                 + [pl.BlockSpec((tm, a.shape[1]), row) for a in acts]
                 + [pl.BlockSpec(w.shape, const)],
        out_specs=pl.BlockSpec((tm, D_MODEL), row),
        out_shape=jax.ShapeDtypeStruct((s, D_MODEL), F32),
        compiler_params=_cparams(("parallel",)),
        name="out_proj",
    )(x, mod3, *acts, w)


def _pad_lanes(v, lane0):
    return jnp.zeros((1, LANES), F32).at[0, lane0:lane0 + v.shape[0]].set(v.astype(F32))


def _swap_halves(w):
    half = w.shape[-1] // 2
    return jnp.concatenate([w[..., half:], w[..., :half]], axis=-1)


def _even_w_in(w):
    qkv, z, beta, a, cq, ckv, kr = jnp.split(
        w, [GDN_CONV_CH, GDN_CONV_CH + GDN_WIDTH, GDN_CONV_CH + GDN_WIDTH + GDN_HEADS,
            GDN_CONV_CH + GDN_WIDTH + 2 * GDN_HEADS, GDN_CONV_CH + GDN_WIDTH + 2 * GDN_HEADS + MLA_Q_RANK,
            GDN_CONV_CH + GDN_WIDTH + 2 * GDN_HEADS + MLA_Q_RANK + MLA_KV_RANK], axis=1)
    pad = jnp.zeros((D_MODEL, LANES - 2 * GDN_HEADS), w.dtype)
    return jnp.concatenate([qkv, z, cq, ckv, kr, _swap_halves(kr), beta, a, pad], axis=1).astype(BF16)


def _even_w_uq(w):
    pe = w[..., MLA_NOPE:]
    return jnp.concatenate([w, _swap_halves(pe)], axis=-1).reshape(MLA_Q_RANK, MLA_HEADS * 2 * LANES).astype(BF16)


def _odd_w_in(w):
    pad = jnp.zeros((D_MODEL, LANES - SSD_HEADS), w.dtype)
    return jnp.concatenate([w, pad], axis=1).astype(BF16)


def kernel(x, c, positions, ada_w, ada_b, norm_g, ffn_w1, ffn_w3, ffn_w2, ev_w_in, gdn_conv_w, gdn_A_log, gdn_dt_bias, gdn_norm_g, mla_q_norm_g, mla_w_uq, mla_kv_norm_g, mla_w_ukv, ev_w_out, ssd_w_in, ssd_conv_w, ssd_conv_b, ssd_A_log, ssd_dt_bias, ssd_D, ssd_norm_g, ssd_w_out, final_g):
    b, s, d = x.shape
    assert b == 1 and d == D_MODEL and s % 256 == 0
    xs = x.reshape(s, d)
    mod = _modulation(c.reshape(d, 1), ada_w, ada_b).reshape(DEPTH, 3, 3, d)
    rope = _rope_table(positions.reshape(s, 1))
    fg = final_g.reshape(1, d)
    for l in range(DEPTH):
        last = l == DEPTH - 1
        xs = _ffn(xs, mod[l, 0], norm_g[l, 0].reshape(1, d), ffn_w1[l, 0].astype(BF16), ffn_w3[l, 0].astype(BF16),
                  ffn_w2[l, 0].astype(BF16), fg, False)
        g1 = norm_g[l, 1].reshape(1, d)
        if l % 2 == 0:
            e = l // 2
            qkv, z, ba, q, k, v = _even_in(
                xs, mod[l, 1], g1, _even_w_in(ev_w_in[e]), gdn_conv_w[e], rope,
                mla_q_norm_g[e].reshape(1, -1), _even_w_uq(mla_w_uq[e]),
                mla_kv_norm_g[e].reshape(1, -1),
                mla_w_ukv[e].reshape(MLA_KV_RANK, MLA_HEADS * 2 * LANES).astype(BF16))
            o_a = _gdn(qkv, ba, z, _pad_lanes(gdn_A_log[e], GDN_HEADS), _pad_lanes(gdn_dt_bias[e], GDN_HEADS),
                       gdn_norm_g[e].reshape(1, -1))
            o_b = _flash(q, k, v)
            xs = _out_proj(xs, mod[l, 1], [o_a, o_b], ev_w_out[e].astype(BF16))
        else:
            o = l // 2
            z, xbc, dt = _odd_in(xs, mod[l, 1], g1, _odd_w_in(ssd_w_in[o]), ssd_conv_w[o],
                                 ssd_conv_b[o].reshape(1, -1))
            y = _ssd(xbc, dt, z, _pad_lanes(ssd_A_log[o], 0), _pad_lanes(ssd_dt_bias[o], 0),
                     jnp.repeat(ssd_D[o].astype(F32), SSD_HEADDIM).reshape(1, -1), ssd_norm_g[o].reshape(1, -1))
            xs = _out_proj(xs, mod[l, 1], [y], ssd_w_out[o].astype(BF16))
        xs = _ffn(xs, mod[l, 2], norm_g[l, 2].reshape(1, d), ffn_w1[l, 1].astype(BF16), ffn_w3[l, 1].astype(BF16),
                  ffn_w2[l, 1].astype(BF16), fg, last)
    return xs.reshape(b, s, d)
```

```python
import functools
import math

import jax
import jax.numpy as jnp
from jax import lax
from jax.experimental import pallas as pl
from jax.experimental.pallas import tpu as pltpu

F32 = jnp.float32
BF16 = jnp.bfloat16

D_MODEL = 1024
DEPTH = 4
CHUNK = 64
NORM_EPS = 1e-6
CONV_K = 4
D_FF = 2816
N_MOD = 9

GDN_HEADS = 4
GDN_DK = 128
GDN_DV = 128
GDN_WIDTH = GDN_HEADS * GDN_DV
GDN_CONV_CH = 2 * GDN_HEADS * GDN_DK + GDN_WIDTH

MLA_HEADS = 4
MLA_Q_RANK = 384
MLA_KV_RANK = 256
MLA_NOPE = 128
MLA_ROPE = 64
MLA_DV = 128
MLA_WIDTH = MLA_HEADS * MLA_DV
MLA_QK = MLA_NOPE + MLA_ROPE
ROPE_THETA = 10000.0

SSD_D_INNER = 2 * D_MODEL
SSD_HEADDIM = 64
SSD_HEADS = SSD_D_INNER // SSD_HEADDIM
SSD_GROUPS = 4
SSD_STATE = 128
SSD_CONV_CH = SSD_D_INNER + 2 * SSD_GROUPS * SSD_STATE
SSD_GROUP_W = SSD_D_INNER // SSD_GROUPS
SSD_HEADS_PER_GROUP = SSD_HEADS // SSD_GROUPS

LANES = 128
SUBLANES = 8
VMEM_LIMIT = 56 * 1024 * 1024

EV_QKV = 0
EV_Z = GDN_CONV_CH
EV_CQ = EV_Z + GDN_WIDTH
EV_CKV = EV_CQ + MLA_Q_RANK
EV_KR = EV_CKV + MLA_KV_RANK
EV_BA = EV_KR + 2 * MLA_ROPE
EV_IN_P = EV_BA + LANES

OD_Z = 0
OD_XBC = SSD_D_INNER
OD_DT = OD_XBC + SSD_CONV_CH
OD_IN_P = OD_DT + LANES


def _cparams(sem):
    return pltpu.CompilerParams(dimension_semantics=sem, vmem_limit_bytes=VMEM_LIMIT)


def _bdot(a, b):
    return jnp.dot(a.astype(BF16), b.astype(BF16), preferred_element_type=F32)


def _bdot_nt(a, b):
    return lax.dot_general(a.astype(BF16), b.astype(BF16), (((1,), (1,)), ((), ())),
                           preferred_element_type=F32)


def _mask_dot(mask, x):
    m = jnp.where(mask, 1.0, 0.0).astype(BF16)
    out = None
    r = x
    for _ in range(3):
        part = r.astype(BF16)
        r = r - part.astype(F32)
        t = jnp.dot(m, part, preferred_element_type=F32)
        out = t if out is None else out + t
    return out


def _silu(x):
    return x * jax.nn.sigmoid(x)


def _softplus(x):
    return jnp.maximum(x, 0.0) + jnp.log1p(jnp.exp(-jnp.abs(x)))


def _adaln(x, g, shift, scale):
    ms = jnp.mean(x * x, axis=-1, keepdims=True)
    y = x * lax.rsqrt(ms + NORM_EPS) * g
    return y * (1.0 + scale) + shift


def _mod_kernel(c_ref, w_ref, b_ref, o_ref):
    ca = _silu(c_ref[...])
    cb = jnp.broadcast_to(ca, (D_MODEL, LANES))
    tn = w_ref.shape[2]
    for t in range(tn // LANES):
        sl = slice(t * LANES, (t + 1) * LANES)
        o_ref[0, :, sl] = jnp.sum(w_ref[0, :, sl] * cb, axis=0, keepdims=True) + b_ref[0, :, sl]


def _modulation(c_col, ada_w, ada_b):
    n = N_MOD * D_MODEL
    tn = n // 4
    return pl.pallas_call(
        _mod_kernel,
        grid=(DEPTH, n // tn),
        in_specs=[pl.BlockSpec((D_MODEL, 1), lambda l, j: (0, 0)),
                  pl.BlockSpec((1, D_MODEL, tn), lambda l, j: (l, 0, j)),
                  pl.BlockSpec((1, 1, tn), lambda l, j: (l, 0, j))],
        out_specs=pl.BlockSpec((1, 1, tn), lambda l, j: (l, 0, j)),
        out_shape=jax.ShapeDtypeStruct((DEPTH, 1, n), F32),
        compiler_params=_cparams(("parallel", "parallel")),
        name="modulation",
    )(c_col, ada_w, ada_b.reshape(DEPTH, 1, n))


def _rope_kernel(pos_ref, freq_ref, o_ref):
    ang = pos_ref[...].astype(F32) * freq_ref[...]
    lane = lax.broadcasted_iota(jnp.int32, ang.shape, 1)
    sgn = jnp.where(lane < 3 * (MLA_ROPE // 2), -1.0, 1.0)
    o_ref[...] = jnp.where(lane < MLA_ROPE, jnp.cos(ang), sgn * jnp.sin(ang))


def _rope_table(pos_col):
    s = pos_col.shape[0]
    tm = min(s, 2048)
    half = MLA_ROPE // 2
    inv_freq = ROPE_THETA ** (-jnp.arange(half, dtype=F32) / half)
    freq = jnp.tile(inv_freq, LANES // half).reshape(1, LANES)
    return pl.pallas_call(
        _rope_kernel,
        grid=(s // tm,),
        in_specs=[pl.BlockSpec((tm, 1), lambda i: (i, 0)),
                  pl.BlockSpec((1, LANES), lambda i: (0, 0))],
        out_specs=pl.BlockSpec((tm, LANES), lambda i: (i, 0)),
        out_shape=jax.ShapeDtypeStruct((s, LANES), F32),
        compiler_params=_cparams(("parallel",)),
        name="rope_table",
    )(pos_col, freq)


def _ffn_kernel(x_ref, mod_ref, g_ref, w1_ref, w3_ref, w2_ref, fg_ref, o_ref, a_ref, *, tf, final):
    x = x_ref[...]
    h = _adaln(x, g_ref[...], mod_ref[0:1, :], mod_ref[1:2, :]).astype(BF16)
    for j in range(D_FF // tf):
        sl = slice(j * tf, (j + 1) * tf)
        h1 = jnp.dot(h, w1_ref[:, sl], preferred_element_type=F32)
        h3 = jnp.dot(h, w3_ref[:, sl], preferred_element_type=F32)
        a_ref[:, sl] = (_silu(h1) * h3).astype(BF16)
    y = jnp.dot(a_ref[...], w2_ref[...], preferred_element_type=F32)
    xn = x + (0.5 * mod_ref[2:3, :]) * y
    if final:
        ms = jnp.mean(xn * xn, axis=-1, keepdims=True)
        xn = xn * lax.rsqrt(ms + NORM_EPS) * fg_ref[...]
    o_ref[...] = xn


def _ffn(x, mod3, g, w1, w3, w2, layer, which, final_g, final):
    s = x.shape[0]
    tm = min(s, 512)
    const = lambda i: (0, 0)
    wsel = lambda i: (layer, which, 0, 0)
    return pl.pallas_call(
        functools.partial(_ffn_kernel, tf=256, final=final),
        grid=(s // tm,),
        in_specs=[pl.BlockSpec((tm, D_MODEL), lambda i: (i, 0)),
                  pl.BlockSpec((3, D_MODEL), const),
                  pl.BlockSpec((1, D_MODEL), const),
                  pl.BlockSpec((None, None, D_MODEL, D_FF), wsel, pipeline_mode=pl.Buffered(1)),
                  pl.BlockSpec((None, None, D_MODEL, D_FF), wsel, pipeline_mode=pl.Buffered(1)),
                  pl.BlockSpec((None, None, D_FF, D_MODEL), wsel, pipeline_mode=pl.Buffered(1)),
                  pl.BlockSpec((1, D_MODEL), const)],
        out_specs=pl.BlockSpec((tm, D_MODEL), lambda i: (i, 0)),
        out_shape=jax.ShapeDtypeStruct((s, D_MODEL), F32),
        scratch_shapes=[pltpu.VMEM((tm, D_FF), BF16)],
        compiler_params=_cparams(("parallel",)),
        name="ffn_final" if final else "ffn",
    )(x, mod3, g, w1, w3, w2, final_g)


def _causal_conv(p, cw, ext_ref, halo_ref, col0):
    tm, w = p.shape
    cs = slice(col0, col0 + w)
    ext_ref[0:SUBLANES, 0:w] = halo_ref[:, cs]
    ext_ref[SUBLANES:SUBLANES + tm, 0:w] = p
    halo_ref[:, cs] = p[tm - SUBLANES:, :]
    y = p * cw[CONV_K - 1:CONV_K, :]
    for j in range(CONV_K - 1):
        off = SUBLANES - (CONV_K - 1) + j
        y = y + ext_ref[off:off + tm, 0:w] * cw[j:j + 1, :]
    return y


def _even_in_kernel(x_ref, mod_ref, g_ref, w_ref, cw_ref, rope_ref, qg_ref, wuq_ref, kvg_ref, wukv_ref,
                    qkv_ref, z_ref, ba_ref, q_ref, k_ref, v_ref, ext_ref, halo_ref):
    @pl.when(pl.program_id(0) == 0)
    def _():
        halo_ref[...] = jnp.zeros_like(halo_ref)

    x = x_ref[...]
    h = _adaln(x, g_ref[...], mod_ref[0:1, :], mod_ref[1:2, :]).astype(BF16)
    proj = lambda c0, w: jnp.dot(h, w_ref[:, c0:c0 + w], preferred_element_type=F32)

    for part in range(3):
        c0 = part * GDN_WIDTH
        y = _silu(_causal_conv(proj(c0, GDN_WIDTH), cw_ref[:, c0:c0 + GDN_WIDTH], ext_ref, halo_ref, c0))
        if part == 2:
            qkv_ref[:, c0:c0 + GDN_WIDTH] = y
        else:
            post = GDN_DK ** -0.5 if part == 0 else 1.0
            for hd in range(GDN_HEADS):
                yh = y[:, hd * GDN_DK:(hd + 1) * GDN_DK]
                nrm = lax.rsqrt(jnp.sum(yh * yh, axis=-1, keepdims=True) + NORM_EPS)
                qkv_ref[:, c0 + hd * GDN_DK:c0 + (hd + 1) * GDN_DK] = yh * nrm * post
    z_ref[...] = proj(EV_Z, GDN_WIDTH)
    ba_ref[...] = proj(EV_BA, LANES)

    rope = rope_ref[...]
    scale = MLA_QK ** -0.5 * math.log2(math.e)

    def rot(xx):
        yy = xx * rope
        return yy + pltpu.roll(yy, MLA_ROPE, 1)

    cq = proj(EV_CQ, MLA_Q_RANK)
    cq = cq * lax.rsqrt(jnp.mean(cq * cq, axis=-1, keepdims=True) + NORM_EPS) * qg_ref[...]
    qf = jnp.dot(cq.astype(BF16), wuq_ref[...], preferred_element_type=F32)
    ckv = proj(EV_CKV, MLA_KV_RANK)
    ckv = ckv * lax.rsqrt(jnp.mean(ckv * ckv, axis=-1, keepdims=True) + NORM_EPS) * kvg_ref[...]
    kvf = jnp.dot(ckv.astype(BF16), wukv_ref[...], preferred_element_type=F32)
    kpe = rot(proj(EV_KR, 2 * MLA_ROPE))[:, 0:MLA_ROPE].astype(BF16)
    for hd in range(MLA_HEADS):
        b0 = hd * 2 * LANES
        q_ref[hd, :, 0:MLA_NOPE] = (qf[:, b0:b0 + MLA_NOPE] * scale).astype(BF16)
        qpe = rot(qf[:, b0 + LANES:b0 + 2 * LANES]) * scale
        q_ref[hd, :, MLA_NOPE:MLA_QK] = qpe[:, 0:MLA_ROPE].astype(BF16)
        k_ref[hd, :, 0:MLA_NOPE] = kvf[:, b0:b0 + MLA_NOPE].astype(BF16)
        k_ref[hd, :, MLA_NOPE:MLA_QK] = kpe
        v_ref[hd] = kvf[:, b0 + LANES:b0 + 2 * LANES].astype(BF16)


def _even_in(x, mod3, g, w_in_p, conv_w, rope, qg, wuq_p, kvg, wukv_p):
    s = x.shape[0]
    tm = min(s, 512)
    const = lambda i: (0, 0)
    row = lambda i: (i, 0)
    hrow = lambda i: (0, i, 0)
    return pl.pallas_call(
        _even_in_kernel,
        grid=(s // tm,),
        in_specs=[pl.BlockSpec((tm, D_MODEL), row),
                  pl.BlockSpec((3, D_MODEL), const),
                  pl.BlockSpec((1, D_MODEL), const),
                  pl.BlockSpec((D_MODEL, EV_IN_P), const),
                  pl.BlockSpec((CONV_K, GDN_CONV_CH), const),
                  pl.BlockSpec((tm, LANES), row),
                  pl.BlockSpec((1, MLA_Q_RANK), const),
                  pl.BlockSpec((MLA_Q_RANK, MLA_HEADS * 2 * LANES), const),
                  pl.BlockSpec((1, MLA_KV_RANK), const),
                  pl.BlockSpec((MLA_KV_RANK, MLA_HEADS * 2 * LANES), const)],
        out_specs=[pl.BlockSpec((tm, GDN_CONV_CH), row),
                   pl.BlockSpec((tm, GDN_WIDTH), row),
                   pl.BlockSpec((tm, LANES), row),
                   pl.BlockSpec((MLA_HEADS, tm, MLA_QK), hrow),
                   pl.BlockSpec((MLA_HEADS, tm, MLA_QK), hrow),
                   pl.BlockSpec((MLA_HEADS, tm, MLA_DV), hrow)],
        out_shape=[jax.ShapeDtypeStruct((s, GDN_CONV_CH), F32),
                   jax.ShapeDtypeStruct((s, GDN_WIDTH), F32),
                   jax.ShapeDtypeStruct((s, LANES), F32),
                   jax.ShapeDtypeStruct((MLA_HEADS, s, MLA_QK), BF16),
                   jax.ShapeDtypeStruct((MLA_HEADS, s, MLA_QK), BF16),
                   jax.ShapeDtypeStruct((MLA_HEADS, s, MLA_DV), BF16)],
        scratch_shapes=[pltpu.VMEM((tm + SUBLANES, GDN_WIDTH), F32),
                        pltpu.VMEM((SUBLANES, GDN_CONV_CH), F32)],
        compiler_params=_cparams(("arbitrary",)),
        name="even_in",
    )(x, mod3, g, w_in_p, conv_w, rope, qg, wuq_p, kvg, wukv_p)


GDN_TB = 256


def _gdn_kernel(qkv_ref, ba_ref, z_ref, alog_ref, dtb_ref, gn_ref, o_ref, s_ref):
    tb = GDN_TB
    nch = tb // CHUNK

    @pl.when(pl.program_id(0) == 0)
    def _():
        s_ref[...] = jnp.zeros_like(s_ref)

    ba = ba_ref[...]
    lane = lax.broadcasted_iota(jnp.int32, (tb, LANES), 1)
    beta_all = jax.nn.sigmoid(ba)
    g_all = jnp.where((lane >= GDN_HEADS) & (lane < 2 * GDN_HEADS),
                      -jnp.exp(alog_ref[...]) * _softplus(ba + dtb_ref[...]), 0.0)
    ri = lax.broadcasted_iota(jnp.int32, (tb, tb), 0)
    ci = lax.broadcasted_iota(jnp.int32, (tb, tb), 1)
    same = (ri // CHUNK) == (ci // CHUNK)
    gc_all = _mask_dot(same & (ci <= ri), g_all)
    gc_t = gc_all.T
    gn = gn_ref[...]
    bd_mask = jnp.where(same, 1.0, 0.0).astype(BF16)

    wi = lax.broadcasted_iota(jnp.int32, (CHUNK, tb), 0)
    wl = lax.broadcasted_iota(jnp.int32, (CHUNK, tb), 1)
    wj, wc = wl % CHUNK, wl // CHUNK
    w_lower, w_strict = wj <= wi, wj < wi
    w_eye = jnp.where(wj == wi, 1.0, 0.0)

    def fold(full):
        out = full[0:CHUNK]
        for c in range(1, nch):
            out = jnp.where(wc == c, full[c * CHUNK:(c + 1) * CHUNK], out)
        return out

    def block_diag(xb):
        return jnp.concatenate([xb] * nch, axis=0) * bd_mask

    def split(x):
        hi = x.astype(BF16)
        return hi, (x - hi.astype(F32)).astype(BF16)

    def wide_dot3(xs, b):
        bh, bl = split(b)
        parts = [split(x) for x in xs]
        r = jnp.dot(jnp.concatenate([t for hl in parts for t in hl], axis=0), block_diag(bh),
                    preferred_element_type=F32)
        r2 = jnp.dot(jnp.concatenate([hl[0] for hl in parts], axis=0), block_diag(bl),
                     preferred_element_type=F32)
        return [r[2 * i * CHUNK:(2 * i + 1) * CHUNK] + r[(2 * i + 1) * CHUNK:(2 * i + 2) * CHUNK]
                + r2[i * CHUNK:(i + 1) * CHUNK] for i in range(len(xs))]

    heads = range(GDN_HEADS)
    hv = []
    for hd in heads:
        gcol = gc_all[:, GDN_HEADS + hd:GDN_HEADS + hd + 1]
        grow = gc_t[GDN_HEADS + hd:GDN_HEADS + hd + 1, :]
        gcol_w = fold(jnp.broadcast_to(gcol, (tb, tb)))
        decay = jnp.where(w_lower, jnp.exp(jnp.where(w_lower, gcol_w - grow, 0.0)), 0.0)
        beta = beta_all[:, hd:hd + 1]
        q = qkv_ref[:, hd * GDN_DK:(hd + 1) * GDN_DK]
        k = qkv_ref[:, GDN_WIDTH + hd * GDN_DK:GDN_WIDTH + (hd + 1) * GDN_DK]
        v = qkv_ref[:, 2 * GDN_WIDTH + hd * GDN_DV:2 * GDN_WIDTH + (hd + 1) * GDN_DV]
        kb = k * beta
        egc = jnp.exp(gcol)
        gl = jnp.concatenate([jnp.broadcast_to(gcol[(c + 1) * CHUNK - 1:(c + 1) * CHUNK, :], (CHUNK, 1))
                              for c in range(nch)], axis=0)
        hv.append(dict(decay=decay, q=q, k=k, kb=kb, qs=q * egc, egl=jnp.exp(gl),
                       rhs=jnp.concatenate([v * beta, kb * egc], axis=1).astype(BF16),
                       kend_t=(k * jnp.exp(gl - gcol)).T))

    pw = [jnp.where(w_strict, -(fold(_bdot_nt(h["kb"], h["k"])) * h["decay"]), 0.0) for h in hv]
    tw = [w_eye + p for p in pw]
    pw = [wide_dot3([p], p)[0] for p in pw]
    for _ in range(4):
        res = [wide_dot3([t, p], p) for t, p in zip(tw, pw)]
        tw = [t + r[0] for t, r in zip(tw, res)]
        pw = [r[1] for r in res]
    tw = [t + wide_dot3([t], p)[0] for t, p in zip(tw, pw)]
    pre = []
    for hd in heads:
        h = hv[hd]
        sol = jnp.dot(block_diag(tw[hd].astype(BF16)), h["rhs"], preferred_element_type=F32)
        attn = fold(_bdot_nt(h["q"], h["k"])) * h["decay"]
        pre.append((sol[:, 0:GDN_DV], sol[:, GDN_DV:], attn, h["qs"], h["kend_t"], h["egl"]))

    col_chunk = lax.broadcasted_iota(jnp.int32, (GDN_DK, tb), 1) // CHUNK
    states = [s_ref[hd] for hd in heads]
    for c in range(nch):
        rs = slice(c * CHUNK, (c + 1) * CHUNK)
        wss = [_bdot(jnp.concatenate([pre[hd][1][rs], pre[hd][3][rs]], axis=0), states[hd]) for hd in heads]
        v_full = []
        for hd in heads:
            parts = [jnp.zeros((CHUNK, GDN_DV), F32)] * nch
            parts[c] = pre[hd][0][rs] - wss[hd][0:CHUNK]
            v_full.append(jnp.concatenate(parts, axis=0))
        states = [states[hd] * pre[hd][5][c * CHUNK:c * CHUNK + 1, :]
                  + _bdot(jnp.where(col_chunk == c, pre[hd][4], 0.0), v_full[hd]) for hd in heads]
        for hd in heads:
            o = wss[hd][CHUNK:] + _bdot(jnp.where(wc == c, pre[hd][2], 0.0), v_full[hd])
            ms = jnp.mean(o * o, axis=-1, keepdims=True)
            zz = z_ref[rs, hd * GDN_DV:(hd + 1) * GDN_DV]
            o_ref[rs, hd * GDN_DV:(hd + 1) * GDN_DV] = (o * lax.rsqrt(ms + NORM_EPS) * gn * _silu(zz)).astype(BF16)
    for hd in heads:
        s_ref[hd] = states[hd]


def _gdn(qkv, ba, z, alog_row, dtb_row, gn):
    s = qkv.shape[0]
    tb = GDN_TB
    const = lambda i: (0, 0)
    row = lambda i: (i, 0)
    return pl.pallas_call(
        _gdn_kernel,
        grid=(s // tb,),
        in_specs=[pl.BlockSpec((tb, GDN_CONV_CH), row),
                  pl.BlockSpec((tb, LANES), row),
                  pl.BlockSpec((tb, GDN_WIDTH), row),
                  pl.BlockSpec((1, LANES), const),
                  pl.BlockSpec((1, LANES), const),
                  pl.BlockSpec((1, GDN_DV), const)],
        out_specs=pl.BlockSpec((tb, GDN_WIDTH), row),
        out_shape=jax.ShapeDtypeStruct((s, GDN_WIDTH), BF16),
        scratch_shapes=[pltpu.VMEM((GDN_HEADS, GDN_DK, GDN_DV), F32)],
        compiler_params=_cparams(("arbitrary",)),
        name="gdn",
    )(qkv, ba, z, alog_row, dtb_row, gn)


def _flash_kernel(q_ref, k_ref, v_ref, o_ref, m_ref, l_ref, acc_ref, *, t):
    qi = pl.program_id(1)
    nrep = t // LANES
    m_ref[...] = jnp.full_like(m_ref, -jnp.inf)
    l_ref[...] = jnp.zeros_like(l_ref)
    acc_ref[...] = jnp.zeros_like(acc_ref)

    def attend(kb, halves):
        start = pl.multiple_of(kb * t, t)
        k = k_ref[0, pl.ds(start, t), :]
        v = v_ref[0, pl.ds(start, t), :]
        scores = []
        for half, diagonal in halves:
            q = q_ref[0, half * t:(half + 1) * t, :]
            s = lax.dot_general(q, k, (((1,), (1,)), ((), ())), preferred_element_type=F32)
            if diagonal:
                qc = lax.broadcasted_iota(jnp.int32, (t, t), 0) // CHUNK
                kc = lax.broadcasted_iota(jnp.int32, (t, t), 1) // CHUNK
                s = jnp.where(kc <= qc, s, -jnp.inf)
            scores.append(s)
        probs = []
        for (half, _), s in zip(halves, scores):
            m_old = m_ref[half]
            m_new = jnp.maximum(m_old, jnp.max(s, axis=-1, keepdims=True))
            alpha = jnp.exp2(m_old - m_new)
            p = jnp.exp2(s - jnp.concatenate([m_new] * nrep, axis=1))
            psum = p[:, 0:LANES]
            for r in range(1, nrep):
                psum = psum + p[:, r * LANES:(r + 1) * LANES]
            l_ref[half] = l_ref[half] * alpha + psum
            m_ref[half] = m_new
            probs.append((alpha, p.astype(BF16)))
        for (half, _), (alpha, p) in zip(halves, probs):
            acc_ref[half] = acc_ref[half] * alpha + jnp.dot(p, v, preferred_element_type=F32)

    def body(i, carry):
        for kb in (2 * i, 2 * i + 1):
            attend(kb, [(0, False), (1, False)])
        return carry

    lax.fori_loop(0, qi, body, 0)
    attend(2 * qi, [(0, True), (1, False)])
    attend(2 * qi + 1, [(1, True)])
    for half in range(2):
        l = jnp.sum(l_ref[half], axis=-1, keepdims=True)
        o_ref[half * t:(half + 1) * t, :] = (acc_ref[half] / l).astype(o_ref.dtype)


def _flash(q, k, v):
    nh, s, _ = q.shape
    t = 512
    assert s % (2 * t) == 0
    return pl.pallas_call(
        functools.partial(_flash_kernel, t=t),
        grid=(nh, s // (2 * t)),
        in_specs=[pl.BlockSpec((1, 2 * t, MLA_QK), lambda h, qi: (h, qi, 0)),
                  pl.BlockSpec((1, s, MLA_QK), lambda h, qi: (h, 0, 0)),
                  pl.BlockSpec((1, s, MLA_DV), lambda h, qi: (h, 0, 0))],
        out_specs=pl.BlockSpec((2 * t, MLA_DV), lambda h, qi: (qi, h)),
        out_shape=jax.ShapeDtypeStruct((s, MLA_WIDTH), BF16),
        scratch_shapes=[pltpu.VMEM((2, t, LANES), F32), pltpu.VMEM((2, t, LANES), F32),
                        pltpu.VMEM((2, t, MLA_DV), F32)],
        compiler_params=_cparams(("parallel", "arbitrary")),
        name="mla_flash",
    )(q, k, v)


def _odd_in_kernel(x_ref, mod_ref, g_ref, w_ref, cw_ref, cb_ref, z_ref, xbc_ref, dt_ref, ext_ref, halo_ref, *, tc):
    @pl.when(pl.program_id(0) == 0)
    def _():
        halo_ref[...] = jnp.zeros_like(halo_ref)

    x = x_ref[...]
    h = _adaln(x, g_ref[...], mod_ref[0:1, :], mod_ref[1:2, :]).astype(BF16)
    proj = lambda c0, w: jnp.dot(h, w_ref[:, c0:c0 + w], preferred_element_type=F32)
    for j in range(SSD_D_INNER // tc):
        z_ref[:, j * tc:(j + 1) * tc] = proj(OD_Z + j * tc, tc)
    for j in range(SSD_CONV_CH // tc):
        c0 = j * tc
        y = _causal_conv(proj(OD_XBC + c0, tc), cw_ref[:, c0:c0 + tc], ext_ref, halo_ref, c0)
        xbc_ref[:, c0:c0 + tc] = _silu(y + cb_ref[:, c0:c0 + tc])
    dt_ref[...] = proj(OD_DT, LANES)


def _odd_in(x, mod3, g, w_in_p, conv_w, conv_b):
    s = x.shape[0]
    tm = min(s, 512)
    tc = 512
    const = lambda i: (0, 0)
    row = lambda i: (i, 0)
    return pl.pallas_call(
        functools.partial(_odd_in_kernel, tc=tc),
        grid=(s // tm,),
        in_specs=[pl.BlockSpec((tm, D_MODEL), row),
                  pl.BlockSpec((3, D_MODEL), const),
                  pl.BlockSpec((1, D_MODEL), const),
                  pl.BlockSpec((D_MODEL, OD_IN_P), const),
                  pl.BlockSpec((CONV_K, SSD_CONV_CH), const),
                  pl.BlockSpec((1, SSD_CONV_CH), const)],
        out_specs=[pl.BlockSpec((tm, SSD_D_INNER), row),
                   pl.BlockSpec((tm, SSD_CONV_CH), row),
                   pl.BlockSpec((tm, LANES), row)],
        out_shape=[jax.ShapeDtypeStruct((s, SSD_D_INNER), F32),
                   jax.ShapeDtypeStruct((s, SSD_CONV_CH), F32),
                   jax.ShapeDtypeStruct((s, LANES), F32)],
        scratch_shapes=[pltpu.VMEM((tm + SUBLANES, tc), F32),
                        pltpu.VMEM((SUBLANES, SSD_CONV_CH), F32)],
        compiler_params=_cparams(("arbitrary",)),
        name="odd_in",
    )(x, mod3, g, w_in_p, conv_w, conv_b)


SSD_TB = 256
SSD_CHUNK = 128


def _ssd_kernel(xbc_ref, dt_ref, z_ref, alog_ref, dtb_ref, dskip_ref, ng_ref, o_ref, hs_ref):
    tb, lc = SSD_TB, SSD_CHUNK

    @pl.when(pl.program_id(0) == 0)
    def _():
        hs_ref[...] = jnp.zeros_like(hs_ref)

    lane = lax.broadcasted_iota(jnp.int32, (tb, LANES), 1)
    dt = jnp.where(lane < SSD_HEADS, _softplus(dt_ref[...] + dtb_ref[...]), 0.0)
    da = dt * (-jnp.exp(alog_ref[...]))
    ri = lax.broadcasted_iota(jnp.int32, (tb, tb), 0)
    ci = lax.broadcasted_iota(jnp.int32, (tb, tb), 1)
    acs = _mask_dot(((ri // lc) == (ci // lc)) & (ci <= ri), da)
    acs_t = acs.T
    left = lax.broadcasted_iota(jnp.int32, (lc, LANES), 1) < SSD_HEADDIM
    causal = jnp.where(lax.broadcasted_iota(jnp.int32, (lc, lc), 1) <= lax.broadcasted_iota(jnp.int32, (lc, lc), 0),
                       0.0, -1e30)
    groups = range(SSD_GROUPS)
    pairs = range(SSD_HEADS_PER_GROUP // 2)

    def pair(arr, h0):
        return jnp.where(left[0:arr.shape[0]], arr[:, h0:h0 + 1], arr[:, h0 + 1:h0 + 2])

    def gcols(g):
        return slice(g * SSD_GROUP_W, (g + 1) * SSD_GROUP_W)

    states = [hs_ref[g] for g in groups]
    for sc in range(tb // lc):
        rs = slice(sc * lc, (sc + 1) * lc)
        acs_c, acs_tc, dt_c = acs[rs], acs_t[:, rs], dt[rs]
        a_last = acs_c[lc - 1:lc, :]
        d_start, d_end, c_decay = jnp.exp(acs_c), jnp.exp(a_last - acs_c), jnp.exp(a_last)
        xs = [xbc_ref[rs, gcols(g)] for g in groups]
        bg = [xbc_ref[rs, SSD_D_INNER + g * SSD_STATE:SSD_D_INNER + (g + 1) * SSD_STATE] for g in groups]
        cg = [xbc_ref[rs, SSD_D_INNER + (SSD_GROUPS + g) * SSD_STATE:SSD_D_INNER + (SSD_GROUPS + g + 1) * SSD_STATE]
              for g in groups]
        cb = [_bdot_nt(cg[g], bg[g]) for g in groups]
        y_off = [_bdot(cg[g], states[g]) for g in groups]
        ys = [[None] * len(pairs) for _ in groups]
        xde = [[None] * len(pairs) for _ in groups]
        cdec = [[None] * len(pairs) for _ in groups]
        for pr in pairs:
            for g in groups:
                h0 = g * SSD_HEADS_PER_GROUP + 2 * pr
                xdt = xs[g][:, pr * LANES:(pr + 1) * LANES] * pair(dt_c, h0)
                acc = y_off[g][:, pr * LANES:(pr + 1) * LANES] * pair(d_start, h0)
                for half in range(2):
                    col = acs_c[:, h0 + half:h0 + half + 1]
                    rowv = acs_tc[h0 + half:h0 + half + 1, :]
                    lmat = jnp.exp(col - rowv + causal)
                    xh = jnp.where(left if half == 0 else jnp.logical_not(left), xdt, 0.0)
                    acc = acc + _bdot(cb[g] * lmat, xh)
                ys[g][pr] = acc
                xde[g][pr] = xdt * pair(d_end, h0)
                cdec[g][pr] = pair(c_decay, h0)
        states = [states[g] * jnp.concatenate(cdec[g], axis=1) + _bdot(bg[g].T, jnp.concatenate(xde[g], axis=1))
                  for g in groups]
        for g in groups:
            y = jnp.concatenate(ys[g], axis=1) + dskip_ref[:, gcols(g)] * xs[g]
            y = y * _silu(z_ref[rs, gcols(g)])
            ms = jnp.mean(y * y, axis=-1, keepdims=True)
            o_ref[rs, gcols(g)] = (y * lax.rsqrt(ms + NORM_EPS) * ng_ref[:, gcols(g)]).astype(BF16)
    for g in groups:
        hs_ref[g] = states[g]


def _ssd(xbc, dt, z, alog_row, dtb_row, dskip_row, ng):
    s = xbc.shape[0]
    tb = SSD_TB
    const = lambda i: (0, 0)
    row = lambda i: (i, 0)
    return pl.pallas_call(
        _ssd_kernel,
        grid=(s // tb,),
        in_specs=[pl.BlockSpec((tb, SSD_CONV_CH), row),
                  pl.BlockSpec((tb, LANES), row),
                  pl.BlockSpec((tb, SSD_D_INNER), row),
                  pl.BlockSpec((1, LANES), const),
                  pl.BlockSpec((1, LANES), const),
                  pl.BlockSpec((1, SSD_D_INNER), const),
                  pl.BlockSpec((1, SSD_D_INNER), const)],
        out_specs=pl.BlockSpec((tb, SSD_D_INNER), row),
        out_shape=jax.ShapeDtypeStruct((s, SSD_D_INNER), BF16),
        scratch_shapes=[pltpu.VMEM((SSD_GROUPS, SSD_STATE, SSD_GROUP_W), F32)],
        compiler_params=_cparams(("arbitrary",)),
        name="ssd",
    )(xbc, dt, z, alog_row, dtb_row, dskip_row, ng)


def _out_kernel(*refs, n_in):
    x_ref, mod_ref = refs[0], refs[1]
    a_refs = refs[2:2 + n_in]
    w_ref = refs[2 + n_in]
    o_ref = refs[3 + n_in]
    y = None
    k0 = 0
    for a_ref in a_refs:
        kw = a_ref.shape[1]
        t = jnp.dot(a_ref[...], w_ref[k0:k0 + kw, :], preferred_element_type=F32)
        y = t if y is None else y + t
        k0 += kw
    o_ref[...] = x_ref[...] + mod_ref[2:3, :] * y


def _out_proj(x, mod3, acts, w):
    s = x.shape[0]
    tm = min(s, 512)
    const = lambda i: (0, 0)
    row = lambda i: (i, 0)
    return pl.pallas_call(
        functools.partial(_out_kernel, n_in=len(acts)),
        grid=(s // tm,),
        in_specs=[pl.BlockSpec((tm, D_MODEL), row), pl.BlockSpec((3, D_MODEL), const)]
                 + [pl.BlockSpec((tm, a.shape[1]), row) for a in acts]
                 + [pl.BlockSpec(w.shape, const)],
        out_specs=pl.BlockSpec((tm, D_MODEL), row),
        out_shape=jax.ShapeDtypeStruct((s, D_MODEL), F32),
        compiler_params=_cparams(("parallel",)),
        name="out_proj",
    )(x, mod3, *acts, w)


def _pad_lanes(v, lane0):
    return jnp.zeros((1, LANES), F32).at[0, lane0:lane0 + v.shape[0]].set(v.astype(F32))


def _swap_halves(w):
    half = w.shape[-1] // 2
    return jnp.concatenate([w[..., half:], w[..., :half]], axis=-1)


def _even_w_in(w):
    qkv, z, beta, a, cq, ckv, kr = jnp.split(
        w, [GDN_CONV_CH, GDN_CONV_CH + GDN_WIDTH, GDN_CONV_CH + GDN_WIDTH + GDN_HEADS,
            GDN_CONV_CH + GDN_WIDTH + 2 * GDN_HEADS, GDN_CONV_CH + GDN_WIDTH + 2 * GDN_HEADS + MLA_Q_RANK,
            GDN_CONV_CH + GDN_WIDTH + 2 * GDN_HEADS + MLA_Q_RANK + MLA_KV_RANK], axis=1)
    pad = jnp.zeros((D_MODEL, LANES - 2 * GDN_HEADS), w.dtype)
    return jnp.concatenate([qkv, z, cq, ckv, kr, _swap_halves(kr), beta, a, pad], axis=1).astype(BF16)


def _even_w_uq(w):
    pe = w[..., MLA_NOPE:]
    return jnp.concatenate([w, _swap_halves(pe)], axis=-1).reshape(MLA_Q_RANK, MLA_HEADS * 2 * LANES).astype(BF16)


def _odd_w_in(w):
    pad = jnp.zeros((D_MODEL, LANES - SSD_HEADS), w.dtype)
    return jnp.concatenate([w, pad], axis=1).astype(BF16)


def kernel(x, c, positions, ada_w, ada_b, norm_g, ffn_w1, ffn_w3, ffn_w2, ev_w_in, gdn_conv_w, gdn_A_log, gdn_dt_bias, gdn_norm_g, mla_q_norm_g, mla_w_uq, mla_kv_norm_g, mla_w_ukv, ev_w_out, ssd_w_in, ssd_conv_w, ssd_conv_b, ssd_A_log, ssd_dt_bias, ssd_D, ssd_norm_g, ssd_w_out, final_g):
    b, s, d = x.shape
    assert b == 1 and d == D_MODEL and s % 256 == 0
    xs = x.reshape(s, d)
    mod = _modulation(c.reshape(d, 1), ada_w, ada_b).reshape(DEPTH, 3, 3, d)
    rope = _rope_table(positions.reshape(s, 1))
    fg = final_g.reshape(1, d)
    w1b, w3b, w2b = ffn_w1.astype(BF16), ffn_w3.astype(BF16), ffn_w2.astype(BF16)
    for l in range(DEPTH):
        last = l == DEPTH - 1
        xs = _ffn(xs, mod[l, 0], norm_g[l, 0].reshape(1, d), w1b, w3b, w2b, l, 0, fg, False)
        g1 = norm_g[l, 1].reshape(1, d)
        if l % 2 == 0:
            e = l // 2
            qkv, z, ba, q, k, v = _even_in(
                xs, mod[l, 1], g1, _even_w_in(ev_w_in[e]), gdn_conv_w[e], rope,
                mla_q_norm_g[e].reshape(1, -1), _even_w_uq(mla_w_uq[e]),
                mla_kv_norm_g[e].reshape(1, -1),
                mla_w_ukv[e].reshape(MLA_KV_RANK, MLA_HEADS * 2 * LANES).astype(BF16))
            o_a = _gdn(qkv, ba, z, _pad_lanes(gdn_A_log[e], GDN_HEADS), _pad_lanes(gdn_dt_bias[e], GDN_HEADS),
                       gdn_norm_g[e].reshape(1, -1))
            o_b = _flash(q, k, v)
            xs = _out_proj(xs, mod[l, 1], [o_a, o_b], ev_w_out[e].astype(BF16))
        else:
            o = l // 2
            z, xbc, dt = _odd_in(xs, mod[l, 1], g1, _odd_w_in(ssd_w_in[o]), ssd_conv_w[o],
                                 ssd_conv_b[o].reshape(1, -1))
            y = _ssd(xbc, dt, z, _pad_lanes(ssd_A_log[o], 0), _pad_lanes(ssd_dt_bias[o], 0),
                     jnp.repeat(ssd_D[o].astype(F32), SSD_HEADDIM).reshape(1, -1), ssd_norm_g[o].reshape(1, -1))
            xs = _out_proj(xs, mod[l, 1], [y], ssd_w_out[o].astype(BF16))
        xs = _ffn(xs, mod[l, 2], norm_g[l, 2].reshape(1, d), w1b, w3b, w2b, l, 1, fg, last)
    return xs.reshape(b, s, d)
```

```python
import functools
import math

import jax
import jax.numpy as jnp
from jax import lax
from jax.experimental import pallas as pl
from jax.experimental.pallas import tpu as pltpu

F32 = jnp.float32
BF16 = jnp.bfloat16

D_MODEL = 1024
DEPTH = 4
CHUNK = 64
NORM_EPS = 1e-6
CONV_K = 4
D_FF = 2816
N_MOD = 9

GDN_HEADS = 4
GDN_DK = 128
GDN_DV = 128
GDN_WIDTH = GDN_HEADS * GDN_DV
GDN_CONV_CH = 2 * GDN_HEADS * GDN_DK + GDN_WIDTH

MLA_HEADS = 4
MLA_Q_RANK = 384
MLA_KV_RANK = 256
MLA_NOPE = 128
MLA_ROPE = 64
MLA_DV = 128
MLA_WIDTH = MLA_HEADS * MLA_DV
MLA_QK = MLA_NOPE + MLA_ROPE
ROPE_THETA = 10000.0

SSD_D_INNER = 2 * D_MODEL
SSD_HEADDIM = 64
SSD_HEADS = SSD_D_INNER // SSD_HEADDIM
SSD_GROUPS = 4
SSD_STATE = 128
SSD_CONV_CH = SSD_D_INNER + 2 * SSD_GROUPS * SSD_STATE
SSD_GROUP_W = SSD_D_INNER // SSD_GROUPS
SSD_HEADS_PER_GROUP = SSD_HEADS // SSD_GROUPS

LANES = 128
SUBLANES = 8
VMEM_LIMIT = 56 * 1024 * 1024
FLASH_T = 512

EV_QKV = 0
EV_Z = GDN_CONV_CH
EV_CQ = EV_Z + GDN_WIDTH
EV_CKV = EV_CQ + MLA_Q_RANK
EV_KR = EV_CKV + MLA_KV_RANK
EV_BA = EV_KR + 2 * MLA_ROPE
EV_IN_P = EV_BA + LANES

OD_Z = 0
OD_XBC = SSD_D_INNER
OD_DT = OD_XBC + SSD_CONV_CH
OD_IN_P = OD_DT + LANES


def _cparams(sem):
    return pltpu.CompilerParams(dimension_semantics=sem, vmem_limit_bytes=VMEM_LIMIT)


def _bdot(a, b):
    return jnp.dot(a.astype(BF16), b.astype(BF16), preferred_element_type=F32)


def _bdot_nt(a, b):
    return lax.dot_general(a.astype(BF16), b.astype(BF16), (((1,), (1,)), ((), ())),
                           preferred_element_type=F32)


def _mask_dot(mask, x):
    m = jnp.where(mask, 1.0, 0.0).astype(BF16)
    out = None
    r = x
    for _ in range(3):
        part = r.astype(BF16)
        r = r - part.astype(F32)
        t = jnp.dot(m, part, preferred_element_type=F32)
        out = t if out is None else out + t
    return out


def _silu(x):
    return x * jax.nn.sigmoid(x)


def _softplus(x):
    return jnp.maximum(x, 0.0) + jnp.log1p(jnp.exp(-jnp.abs(x)))


def _adaln(x, g, shift, scale):
    ms = jnp.mean(x * x, axis=-1, keepdims=True)
    y = x * lax.rsqrt(ms + NORM_EPS) * g
    return y * (1.0 + scale) + shift


def _mod_kernel(c_ref, w_ref, b_ref, o_ref):
    ca = _silu(c_ref[...])
    cb = jnp.broadcast_to(ca, (D_MODEL, LANES))
    tn = w_ref.shape[2]
    for t in range(tn // LANES):
        sl = slice(t * LANES, (t + 1) * LANES)
        o_ref[0, :, sl] = jnp.sum(w_ref[0, :, sl] * cb, axis=0, keepdims=True) + b_ref[0, :, sl]


def _modulation(c_col, ada_w, ada_b):
    n = N_MOD * D_MODEL
    tn = n // 4
    return pl.pallas_call(
        _mod_kernel,
        grid=(DEPTH, n // tn),
        in_specs=[pl.BlockSpec((D_MODEL, 1), lambda l, j: (0, 0)),
                  pl.BlockSpec((1, D_MODEL, tn), lambda l, j: (l, 0, j)),
                  pl.BlockSpec((1, 1, tn), lambda l, j: (l, 0, j))],
        out_specs=pl.BlockSpec((1, 1, tn), lambda l, j: (l, 0, j)),
        out_shape=jax.ShapeDtypeStruct((DEPTH, 1, n), F32),
        compiler_params=_cparams(("parallel", "parallel")),
        name="modulation",
    )(c_col, ada_w, ada_b.reshape(DEPTH, 1, n))


def _rope_kernel(pos_ref, freq_ref, o_ref):
    ang = pos_ref[...].astype(F32) * freq_ref[...]
    lane = lax.broadcasted_iota(jnp.int32, ang.shape, 1)
    sgn = jnp.where(lane < 3 * (MLA_ROPE // 2), -1.0, 1.0)
    o_ref[...] = jnp.where(lane < MLA_ROPE, jnp.cos(ang), sgn * jnp.sin(ang))


def _rope_table(pos_col):
    s = pos_col.shape[0]
    tm = min(s, 2048)
    half = MLA_ROPE // 2
    inv_freq = ROPE_THETA ** (-jnp.arange(half, dtype=F32) / half)
    freq = jnp.tile(inv_freq, LANES // half).reshape(1, LANES)
    return pl.pallas_call(
        _rope_kernel,
        grid=(s // tm,),
        in_specs=[pl.BlockSpec((tm, 1), lambda i: (i, 0)),
                  pl.BlockSpec((1, LANES), lambda i: (0, 0))],
        out_specs=pl.BlockSpec((tm, LANES), lambda i: (i, 0)),
        out_shape=jax.ShapeDtypeStruct((s, LANES), F32),
        compiler_params=_cparams(("parallel",)),
        name="rope_table",
    )(pos_col, freq)


def _ffn_kernel(x_ref, mod_ref, g_ref, w1_ref, w3_ref, w2_ref, fg_ref, o_ref, a_ref, *, tf, final):
    x = x_ref[...]
    h = _adaln(x, g_ref[...], mod_ref[0:1, :], mod_ref[1:2, :]).astype(BF16)
    for j in range(D_FF // tf):
        sl = slice(j * tf, (j + 1) * tf)
        h1 = jnp.dot(h, w1_ref[:, sl], preferred_element_type=F32)
        h3 = jnp.dot(h, w3_ref[:, sl], preferred_element_type=F32)
        a_ref[:, sl] = (_silu(h1) * h3).astype(BF16)
    y = jnp.dot(a_ref[...], w2_ref[...], preferred_element_type=F32)
    xn = x + (0.5 * mod_ref[2:3, :]) * y
    if final:
        ms = jnp.mean(xn * xn, axis=-1, keepdims=True)
        xn = xn * lax.rsqrt(ms + NORM_EPS) * fg_ref[...]
    o_ref[...] = xn


def _ffn(x, mod3, g, w1, w3, w2, layer, which, final_g, final):
    s = x.shape[0]
    tm = min(s, 512)
    const = lambda i: (0, 0)
    wsel = lambda i: (layer, which, 0, 0)
    return pl.pallas_call(
        functools.partial(_ffn_kernel, tf=256, final=final),
        grid=(s // tm,),
        in_specs=[pl.BlockSpec((tm, D_MODEL), lambda i: (i, 0)),
                  pl.BlockSpec((3, D_MODEL), const),
                  pl.BlockSpec((1, D_MODEL), const),
                  pl.BlockSpec((None, None, D_MODEL, D_FF), wsel, pipeline_mode=pl.Buffered(1)),
                  pl.BlockSpec((None, None, D_MODEL, D_FF), wsel, pipeline_mode=pl.Buffered(1)),
                  pl.BlockSpec((None, None, D_FF, D_MODEL), wsel, pipeline_mode=pl.Buffered(1)),
                  pl.BlockSpec((1, D_MODEL), const)],
        out_specs=pl.BlockSpec((tm, D_MODEL), lambda i: (i, 0)),
        out_shape=jax.ShapeDtypeStruct((s, D_MODEL), F32),
        scratch_shapes=[pltpu.VMEM((tm, D_FF), BF16)],
        compiler_params=_cparams(("parallel",)),
        name="ffn_final" if final else "ffn",
    )(x, mod3, g, w1, w3, w2, final_g)


def _causal_conv(p, cw, ext_ref, halo_ref, col0):
    tm, w = p.shape
    cs = slice(col0, col0 + w)
    ext_ref[0:SUBLANES, 0:w] = halo_ref[:, cs]
    ext_ref[SUBLANES:SUBLANES + tm, 0:w] = p
    halo_ref[:, cs] = p[tm - SUBLANES:, :]
    y = p * cw[CONV_K - 1:CONV_K, :]
    for j in range(CONV_K - 1):
        off = SUBLANES - (CONV_K - 1) + j
        y = y + ext_ref[off:off + tm, 0:w] * cw[j:j + 1, :]
    return y


def _even_in_kernel(x_ref, mod_ref, g_ref, w_ref, cw_ref, rope_ref, qg_ref, wuq_ref, kvg_ref, wukv_ref,
                    qkv_ref, z_ref, ba_ref, q_ref, k_ref, v_ref, ext_ref, halo_ref):
    @pl.when(pl.program_id(0) == 0)
    def _():
        halo_ref[...] = jnp.zeros_like(halo_ref)

    x = x_ref[...]
    h = _adaln(x, g_ref[...], mod_ref[0:1, :], mod_ref[1:2, :]).astype(BF16)
    proj = lambda c0, w: jnp.dot(h, w_ref[:, c0:c0 + w], preferred_element_type=F32)

    for part in range(3):
        c0 = part * GDN_WIDTH
        y = _silu(_causal_conv(proj(c0, GDN_WIDTH), cw_ref[:, c0:c0 + GDN_WIDTH], ext_ref, halo_ref, c0))
        if part == 2:
            qkv_ref[:, c0:c0 + GDN_WIDTH] = y
        else:
            post = GDN_DK ** -0.5 if part == 0 else 1.0
            for hd in range(GDN_HEADS):
                yh = y[:, hd * GDN_DK:(hd + 1) * GDN_DK]
                nrm = lax.rsqrt(jnp.sum(yh * yh, axis=-1, keepdims=True) + NORM_EPS)
                qkv_ref[:, c0 + hd * GDN_DK:c0 + (hd + 1) * GDN_DK] = yh * nrm * post
    z_ref[...] = proj(EV_Z, GDN_WIDTH)
    ba_ref[...] = proj(EV_BA, LANES)

    rope = rope_ref[...]
    scale = MLA_QK ** -0.5 * math.log2(math.e)

    def rot(xx):
        yy = xx * rope
        return yy + pltpu.roll(yy, MLA_ROPE, 1)

    cq = proj(EV_CQ, MLA_Q_RANK)
    cq = cq * lax.rsqrt(jnp.mean(cq * cq, axis=-1, keepdims=True) + NORM_EPS) * qg_ref[...]
    qf = jnp.dot(cq.astype(BF16), wuq_ref[...], preferred_element_type=F32)
    ckv = proj(EV_CKV, MLA_KV_RANK)
    ckv = ckv * lax.rsqrt(jnp.mean(ckv * ckv, axis=-1, keepdims=True) + NORM_EPS) * kvg_ref[...]
    kvf = jnp.dot(ckv.astype(BF16), wukv_ref[...], preferred_element_type=F32)
    kpe = rot(proj(EV_KR, 2 * MLA_ROPE))[:, 0:MLA_ROPE].astype(BF16)
    for hd in range(MLA_HEADS):
        b0 = hd * 2 * LANES
        q_ref[hd, :, 0:MLA_NOPE] = (qf[:, b0:b0 + MLA_NOPE] * scale).astype(BF16)
        qpe = rot(qf[:, b0 + LANES:b0 + 2 * LANES]) * scale
        q_ref[hd, :, MLA_NOPE:MLA_QK] = qpe[:, 0:MLA_ROPE].astype(BF16)
        k_ref[hd, :, 0:MLA_NOPE] = kvf[:, b0:b0 + MLA_NOPE].astype(BF16)
        k_ref[hd, :, MLA_NOPE:MLA_QK] = kpe
        v_ref[hd] = kvf[:, b0 + LANES:b0 + 2 * LANES].astype(BF16)


def _even_in(x, mod3, g, w_in_p, conv_w, rope, qg, wuq_p, kvg, wukv_p):
    s = x.shape[0]
    tm = min(s, 512)
    const = lambda i: (0, 0)
    row = lambda i: (i, 0)
    hrow = lambda i: (0, i, 0)
    return pl.pallas_call(
        _even_in_kernel,
        grid=(s // tm,),
        in_specs=[pl.BlockSpec((tm, D_MODEL), row),
                  pl.BlockSpec((3, D_MODEL), const),
                  pl.BlockSpec((1, D_MODEL), const),
                  pl.BlockSpec((D_MODEL, EV_IN_P), const),
                  pl.BlockSpec((CONV_K, GDN_CONV_CH), const),
                  pl.BlockSpec((tm, LANES), row),
                  pl.BlockSpec((1, MLA_Q_RANK), const),
                  pl.BlockSpec((MLA_Q_RANK, MLA_HEADS * 2 * LANES), const),
                  pl.BlockSpec((1, MLA_KV_RANK), const),
                  pl.BlockSpec((MLA_KV_RANK, MLA_HEADS * 2 * LANES), const)],
        out_specs=[pl.BlockSpec((tm, GDN_CONV_CH), row),
                   pl.BlockSpec((tm, GDN_WIDTH), row),
                   pl.BlockSpec((tm, LANES), row),
                   pl.BlockSpec((MLA_HEADS, tm, MLA_QK), hrow),
                   pl.BlockSpec((MLA_HEADS, tm, MLA_QK), hrow),
                   pl.BlockSpec((MLA_HEADS, tm, MLA_DV), hrow)],
        out_shape=[jax.ShapeDtypeStruct((s, GDN_CONV_CH), F32),
                   jax.ShapeDtypeStruct((s, GDN_WIDTH), F32),
                   jax.ShapeDtypeStruct((s, LANES), F32),
                   jax.ShapeDtypeStruct((MLA_HEADS, s, MLA_QK), BF16),
                   jax.ShapeDtypeStruct((MLA_HEADS, s, MLA_QK), BF16),
                   jax.ShapeDtypeStruct((MLA_HEADS, s, MLA_DV), BF16)],
        scratch_shapes=[pltpu.VMEM((tm + SUBLANES, GDN_WIDTH), F32),
                        pltpu.VMEM((SUBLANES, GDN_CONV_CH), F32)],
        compiler_params=_cparams(("arbitrary",)),
        name="even_in",
    )(x, mod3, g, w_in_p, conv_w, rope, qg, wuq_p, kvg, wukv_p)


GDN_TB = 256


def _gdn_kernel(qkv_ref, ba_ref, z_ref, alog_ref, dtb_ref, gn_ref, o_ref, s_ref):
    tb = GDN_TB
    nch = tb // CHUNK

    @pl.when(pl.program_id(0) == 0)
    def _():
        s_ref[...] = jnp.zeros_like(s_ref)

    ba = ba_ref[...]
    lane = lax.broadcasted_iota(jnp.int32, (tb, LANES), 1)
    beta_all = jax.nn.sigmoid(ba)
    g_all = jnp.where((lane >= GDN_HEADS) & (lane < 2 * GDN_HEADS),
                      -jnp.exp(alog_ref[...]) * _softplus(ba + dtb_ref[...]), 0.0)
    ri = lax.broadcasted_iota(jnp.int32, (tb, tb), 0)
    ci = lax.broadcasted_iota(jnp.int32, (tb, tb), 1)
    same = (ri // CHUNK) == (ci // CHUNK)
    gc_all = _mask_dot(same & (ci <= ri), g_all)
    gc_t = gc_all.T
    gn = gn_ref[...]
    bd_mask = jnp.where(same, 1.0, 0.0).astype(BF16)

    wi = lax.broadcasted_iota(jnp.int32, (CHUNK, tb), 0)
    wl = lax.broadcasted_iota(jnp.int32, (CHUNK, tb), 1)
    wj, wc = wl % CHUNK, wl // CHUNK
    w_lower, w_strict = wj <= wi, wj < wi
    w_eye = jnp.where(wj == wi, 1.0, 0.0)

    def fold(full):
        out = full[0:CHUNK]
        for c in range(1, nch):
            out = jnp.where(wc == c, full[c * CHUNK:(c + 1) * CHUNK], out)
        return out

    def block_diag(xb):
        return jnp.concatenate([xb] * nch, axis=0) * bd_mask

    def split(x):
        hi = x.astype(BF16)
        return hi, (x - hi.astype(F32)).astype(BF16)

    def wide_dot3(xs, b):
        bh, bl = split(b)
        parts = [split(x) for x in xs]
        r = jnp.dot(jnp.concatenate([t for hl in parts for t in hl], axis=0), block_diag(bh),
                    preferred_element_type=F32)
        r2 = jnp.dot(jnp.concatenate([hl[0] for hl in parts], axis=0), block_diag(bl),
                     preferred_element_type=F32)
        return [r[2 * i * CHUNK:(2 * i + 1) * CHUNK] + r[(2 * i + 1) * CHUNK:(2 * i + 2) * CHUNK]
                + r2[i * CHUNK:(i + 1) * CHUNK] for i in range(len(xs))]

    heads = range(GDN_HEADS)
    hv = []
    for hd in heads:
        gcol = gc_all[:, GDN_HEADS + hd:GDN_HEADS + hd + 1]
        grow = gc_t[GDN_HEADS + hd:GDN_HEADS + hd + 1, :]
        gcol_w = fold(jnp.broadcast_to(gcol, (tb, tb)))
        decay = jnp.where(w_lower, jnp.exp(jnp.where(w_lower, gcol_w - grow, 0.0)), 0.0)
        beta = beta_all[:, hd:hd + 1]
        q = qkv_ref[:, hd * GDN_DK:(hd + 1) * GDN_DK]
        k = qkv_ref[:, GDN_WIDTH + hd * GDN_DK:GDN_WIDTH + (hd + 1) * GDN_DK]
        v = qkv_ref[:, 2 * GDN_WIDTH + hd * GDN_DV:2 * GDN_WIDTH + (hd + 1) * GDN_DV]
        kb = k * beta
        egc = jnp.exp(gcol)
        gl = jnp.concatenate([jnp.broadcast_to(gcol[(c + 1) * CHUNK - 1:(c + 1) * CHUNK, :], (CHUNK, 1))
                              for c in range(nch)], axis=0)
        hv.append(dict(decay=decay, q=q, k=k, kb=kb, qs=q * egc, egl=jnp.exp(gl),
                       rhs=jnp.concatenate([v * beta, kb * egc], axis=1).astype(BF16),
                       kend_t=(k * jnp.exp(gl - gcol)).T))

    pw = [jnp.where(w_strict, -(fold(_bdot_nt(h["kb"], h["k"])) * h["decay"]), 0.0) for h in hv]
    tw = [w_eye + p for p in pw]
    pw = [wide_dot3([p], p)[0] for p in pw]
    for _ in range(4):
        res = [wide_dot3([t, p], p) for t, p in zip(tw, pw)]
        tw = [t + r[0] for t, r in zip(tw, res)]
        pw = [r[1] for r in res]
    tw = [t + wide_dot3([t], p)[0] for t, p in zip(tw, pw)]
    pre = []
    for hd in heads:
        h = hv[hd]
        sol = jnp.dot(block_diag(tw[hd].astype(BF16)), h["rhs"], preferred_element_type=F32)
        attn = fold(_bdot_nt(h["q"], h["k"])) * h["decay"]
        pre.append((sol[:, 0:GDN_DV], sol[:, GDN_DV:], attn, h["qs"], h["kend_t"], h["egl"]))

    col_chunk = lax.broadcasted_iota(jnp.int32, (GDN_DK, tb), 1) // CHUNK
    states = [s_ref[hd] for hd in heads]
    for c in range(nch):
        rs = slice(c * CHUNK, (c + 1) * CHUNK)
        wss = [_bdot(jnp.concatenate([pre[hd][1][rs], pre[hd][3][rs]], axis=0), states[hd]) for hd in heads]
        v_full = []
        for hd in heads:
            parts = [jnp.zeros((CHUNK, GDN_DV), F32)] * nch
            parts[c] = pre[hd][0][rs] - wss[hd][0:CHUNK]
            v_full.append(jnp.concatenate(parts, axis=0))
        states = [states[hd] * pre[hd][5][c * CHUNK:c * CHUNK + 1, :]
                  + _bdot(jnp.where(col_chunk == c, pre[hd][4], 0.0), v_full[hd]) for hd in heads]
        for hd in heads:
            o = wss[hd][CHUNK:] + _bdot(jnp.where(wc == c, pre[hd][2], 0.0), v_full[hd])
            ms = jnp.mean(o * o, axis=-1, keepdims=True)
            zz = z_ref[rs, hd * GDN_DV:(hd + 1) * GDN_DV]
            o_ref[rs, hd * GDN_DV:(hd + 1) * GDN_DV] = (o * lax.rsqrt(ms + NORM_EPS) * gn * _silu(zz)).astype(BF16)
    for hd in heads:
        s_ref[hd] = states[hd]


def _gdn(qkv, ba, z, alog_row, dtb_row, gn):
    s = qkv.shape[0]
    tb = GDN_TB
    const = lambda i: (0, 0)
    row = lambda i: (i, 0)
    return pl.pallas_call(
        _gdn_kernel,
        grid=(s // tb,),
        in_specs=[pl.BlockSpec((tb, GDN_CONV_CH), row),
                  pl.BlockSpec((tb, LANES), row),
                  pl.BlockSpec((tb, GDN_WIDTH), row),
                  pl.BlockSpec((1, LANES), const),
                  pl.BlockSpec((1, LANES), const),
                  pl.BlockSpec((1, GDN_DV), const)],
        out_specs=pl.BlockSpec((tb, GDN_WIDTH), row),
        out_shape=jax.ShapeDtypeStruct((s, GDN_WIDTH), BF16),
        scratch_shapes=[pltpu.VMEM((GDN_HEADS, GDN_DK, GDN_DV), F32)],
        compiler_params=_cparams(("arbitrary",)),
        name="gdn",
    )(qkv, ba, z, alog_row, dtb_row, gn)


def _flash_kernel(q_ref, k_ref, v_ref, o_ref, m_ref, l_ref, acc_ref):
    t = FLASH_T
    qi = pl.program_id(1)
    nrep = t // LANES
    m_ref[...] = jnp.full_like(m_ref, -jnp.inf)
    l_ref[...] = jnp.zeros_like(l_ref)
    acc_ref[...] = jnp.zeros_like(acc_ref)

    def attend(kb, halves):
        start = pl.multiple_of(kb * t, t)
        k = k_ref[0, pl.ds(start, t), :]
        v = v_ref[0, pl.ds(start, t), :]
        scores = []
        for half, diagonal in halves:
            q = q_ref[0, half * t:(half + 1) * t, :]
            s = lax.dot_general(q, k, (((1,), (1,)), ((), ())), preferred_element_type=F32)
            if diagonal:
                qc = lax.broadcasted_iota(jnp.int32, (t, t), 0) // CHUNK
                kc = lax.broadcasted_iota(jnp.int32, (t, t), 1) // CHUNK
                s = jnp.where(kc <= qc, s, -jnp.inf)
            scores.append(s)
        probs = []
        for (half, _), s in zip(halves, scores):
            m_old = m_ref[half]
            m_new = jnp.maximum(m_old, jnp.max(s, axis=-1, keepdims=True))
            alpha = jnp.exp2(m_old - m_new)
            p = jnp.exp2(s - jnp.concatenate([m_new] * nrep, axis=1))
            psum = p[:, 0:LANES]
            for r in range(1, nrep):
                psum = psum + p[:, r * LANES:(r + 1) * LANES]
            l_ref[half] = l_ref[half] * alpha + psum
            m_ref[half] = m_new
            probs.append((alpha, p.astype(BF16)))
        for (half, _), (alpha, p) in zip(halves, probs):
            acc_ref[half] = acc_ref[half] * alpha + jnp.dot(p, v, preferred_element_type=F32)

    def body(i, carry):
        for kb in range(4):
            attend(4 * i + kb, [(0, False), (1, False)])
        return carry

    lax.fori_loop(0, qi // 2, body, 0)

    @pl.when(qi % 2 == 1)
    def _():
        for kb in (2 * qi - 2, 2 * qi - 1):
            attend(kb, [(0, False), (1, False)])

    attend(2 * qi, [(0, True), (1, False)])
    attend(2 * qi + 1, [(1, True)])
    for half in range(2):
        l = jnp.sum(l_ref[half], axis=-1, keepdims=True)
        o_ref[half * t:(half + 1) * t, :] = (acc_ref[half] / l).astype(o_ref.dtype)


def _flash(q, k, v):
    nh, s, _ = q.shape
    t = FLASH_T
    assert s % (2 * t) == 0
    return pl.pallas_call(
        _flash_kernel,
        grid=(nh, s // (2 * t)),
        in_specs=[pl.BlockSpec((1, 2 * t, MLA_QK), lambda h, qi: (h, qi, 0)),
                  pl.BlockSpec((1, s, MLA_QK), lambda h, qi: (h, 0, 0)),
                  pl.BlockSpec((1, s, MLA_DV), lambda h, qi: (h, 0, 0))],
        out_specs=pl.BlockSpec((2 * t, MLA_DV), lambda h, qi: (qi, h)),
        out_shape=jax.ShapeDtypeStruct((s, MLA_WIDTH), BF16),
        scratch_shapes=[pltpu.VMEM((2, t, LANES), F32), pltpu.VMEM((2, t, LANES), F32),
                        pltpu.VMEM((2, t, MLA_DV), F32)],
        compiler_params=_cparams(("parallel", "arbitrary")),
        name="mla_flash",
    )(q, k, v)


def _odd_in_kernel(x_ref, mod_ref, g_ref, w_ref, cw_ref, cb_ref, z_ref, xbc_ref, dt_ref, ext_ref, halo_ref, *, tc):
    @pl.when(pl.program_id(0) == 0)
    def _():
        halo_ref[...] = jnp.zeros_like(halo_ref)

    x = x_ref[...]
    h = _adaln(x, g_ref[...], mod_ref[0:1, :], mod_ref[1:2, :]).astype(BF16)
    proj = lambda c0, w: jnp.dot(h, w_ref[:, c0:c0 + w], preferred_element_type=F32)
    for j in range(SSD_D_INNER // tc):
        z_ref[:, j * tc:(j + 1) * tc] = proj(OD_Z + j * tc, tc)
    for j in range(SSD_CONV_CH // tc):
        c0 = j * tc
        y = _causal_conv(proj(OD_XBC + c0, tc), cw_ref[:, c0:c0 + tc], ext_ref, halo_ref, c0)
        xbc_ref[:, c0:c0 + tc] = _silu(y + cb_ref[:, c0:c0 + tc])
    dt_ref[...] = proj(OD_DT, LANES)


def _odd_in(x, mod3, g, w_in_p, conv_w, conv_b):
    s = x.shape[0]
    tm = min(s, 512)
    tc = 512
    const = lambda i: (0, 0)
    row = lambda i: (i, 0)
    return pl.pallas_call(
        functools.partial(_odd_in_kernel, tc=tc),
        grid=(s // tm,),
        in_specs=[pl.BlockSpec((tm, D_MODEL), row),
                  pl.BlockSpec((3, D_MODEL), const),
                  pl.BlockSpec((1, D_MODEL), const),
                  pl.BlockSpec((D_MODEL, OD_IN_P), const),
                  pl.BlockSpec((CONV_K, SSD_CONV_CH), const),
                  pl.BlockSpec((1, SSD_CONV_CH), const)],
        out_specs=[pl.BlockSpec((tm, SSD_D_INNER), row),
                   pl.BlockSpec((tm, SSD_CONV_CH), row),
                   pl.BlockSpec((tm, LANES), row)],
        out_shape=[jax.ShapeDtypeStruct((s, SSD_D_INNER), F32),
                   jax.ShapeDtypeStruct((s, SSD_CONV_CH), F32),
                   jax.ShapeDtypeStruct((s, LANES), F32)],
        scratch_shapes=[pltpu.VMEM((tm + SUBLANES, tc), F32),
                        pltpu.VMEM((SUBLANES, SSD_CONV_CH), F32)],
        compiler_params=_cparams(("arbitrary",)),
        name="odd_in",
    )(x, mod3, g, w_in_p, conv_w, conv_b)


SSD_TB = 256
SSD_CHUNK = 128


def _ssd_kernel(xbc_ref, dt_ref, z_ref, alog_ref, dtb_ref, dskip_ref, ng_ref, o_ref, hs_ref):
    tb, lc = SSD_TB, SSD_CHUNK

    @pl.when(pl.program_id(0) == 0)
    def _():
        hs_ref[...] = jnp.zeros_like(hs_ref)

    lane = lax.broadcasted_iota(jnp.int32, (tb, LANES), 1)
    dt = jnp.where(lane < SSD_HEADS, _softplus(dt_ref[...] + dtb_ref[...]), 0.0)
    da = dt * (-jnp.exp(alog_ref[...]))
    ri = lax.broadcasted_iota(jnp.int32, (tb, tb), 0)
    ci = lax.broadcasted_iota(jnp.int32, (tb, tb), 1)
    acs = _mask_dot(((ri // lc) == (ci // lc)) & (ci <= ri), da)
    acs = acs * math.log2(math.e)
    acs_t = acs.T
    ldt_t = jnp.log2(jnp.where(lane < SSD_HEADS, dt, 1.0)).T
    left = lax.broadcasted_iota(jnp.int32, (lc, LANES), 1) < SSD_HEADDIM
    halves = (left, jnp.logical_not(left))
    causal = jnp.where(lax.broadcasted_iota(jnp.int32, (lc, lc), 1) <= lax.broadcasted_iota(jnp.int32, (lc, lc), 0),
                       0.0, -1e30)
    groups = range(SSD_GROUPS)
    pairs = range(SSD_HEADS_PER_GROUP // 2)

    def gcols(g):
        return slice(g * SSD_GROUP_W, (g + 1) * SSD_GROUP_W)

    def pcols(pr):
        return slice(pr * LANES, (pr + 1) * LANES)

    states = [[hs_ref[g, :, pcols(pr)] for pr in pairs] for g in groups]
    for sc in range(tb // lc):
        rs = slice(sc * lc, (sc + 1) * lc)
        acs_c, acs_tc, ldt_tc = acs[rs], acs_t[:, rs], ldt_t[:, rs]
        a_last = acs_c[lc - 1:lc, :]
        c_decay = jnp.exp2(a_last)
        xs = [xbc_ref[rs, gcols(g)] for g in groups]
        bg_t = [xbc_ref[rs, SSD_D_INNER + g * SSD_STATE:SSD_D_INNER + (g + 1) * SSD_STATE].T for g in groups]
        cg = [xbc_ref[rs, SSD_D_INNER + (SSD_GROUPS + g) * SSD_STATE:SSD_D_INNER + (SSD_GROUPS + g + 1) * SSD_STATE]
              for g in groups]
        cb = [lax.dot_general(cg[g].astype(BF16), bg_t[g].astype(BF16), (((1,), (0,)), ((), ())),
                              preferred_element_type=F32) for g in groups]
        ys = [[None] * len(pairs) for _ in groups]
        new_states = [[None] * len(pairs) for _ in groups]
        for pr in pairs:
            for g in groups:
                x_pair = xs[g][:, pcols(pr)]
                st = states[g][pr]
                acc = None
                upd = None
                for half in range(2):
                    hd = g * SSD_HEADS_PER_GROUP + 2 * pr + half
                    colb = jnp.broadcast_to(acs_c[:, hd:hd + 1], (lc, LANES))
                    lrow = ldt_tc[hd:hd + 1, :] - acs_tc[hd:hd + 1, :]
                    m1 = cb[g] * jnp.exp2(colb + (lrow + causal))
                    c1 = cg[g] * jnp.exp2(colb)
                    xh = jnp.where(halves[half], x_pair, 0.0).astype(BF16)
                    sh = jnp.where(halves[half], st, 0.0).astype(BF16)
                    t = jnp.dot(jnp.concatenate([m1, c1], axis=1).astype(BF16), jnp.concatenate([xh, sh], axis=0),
                                preferred_element_type=F32)
                    acc = t if acc is None else acc + t
                    wrow = jnp.exp2(a_last[:, hd:hd + 1] + lrow)
                    u = jnp.dot((bg_t[g] * wrow).astype(BF16), xh, preferred_element_type=F32)
                    upd = u if upd is None else upd + u
                ys[g][pr] = acc
                h0 = g * SSD_HEADS_PER_GROUP + 2 * pr
                cdec = jnp.where(left[0:1], c_decay[:, h0:h0 + 1], c_decay[:, h0 + 1:h0 + 2])
                new_states[g][pr] = st * cdec + upd
        states = new_states
        for g in groups:
            y = jnp.concatenate(ys[g], axis=1) + dskip_ref[:, gcols(g)] * xs[g]
            y = y * _silu(z_ref[rs, gcols(g)])
            ms = jnp.mean(y * y, axis=-1, keepdims=True)
            o_ref[rs, gcols(g)] = (y * lax.rsqrt(ms + NORM_EPS) * ng_ref[:, gcols(g)]).astype(BF16)
    for g in groups:
        for pr in pairs:
            hs_ref[g, :, pcols(pr)] = states[g][pr]


def _ssd(xbc, dt, z, alog_row, dtb_row, dskip_row, ng):
    s = xbc.shape[0]
    tb = SSD_TB
    const = lambda i: (0, 0)
    row = lambda i: (i, 0)
    return pl.pallas_call(
        _ssd_kernel,
        grid=(s // tb,),
        in_specs=[pl.BlockSpec((tb, SSD_CONV_CH), row),
                  pl.BlockSpec((tb, LANES), row),
                  pl.BlockSpec((tb, SSD_D_INNER), row),
                  pl.BlockSpec((1, LANES), const),
                  pl.BlockSpec((1, LANES), const),
                  pl.BlockSpec((1, SSD_D_INNER), const),
                  pl.BlockSpec((1, SSD_D_INNER), const)],
        out_specs=pl.BlockSpec((tb, SSD_D_INNER), row),
        out_shape=jax.ShapeDtypeStruct((s, SSD_D_INNER), BF16),
        scratch_shapes=[pltpu.VMEM((SSD_GROUPS, SSD_STATE, SSD_GROUP_W), F32)],
        compiler_params=_cparams(("arbitrary",)),
        name="ssd",
    )(xbc, dt, z, alog_row, dtb_row, dskip_row, ng)


def _out_kernel(*refs, n_in):
    x_ref, mod_ref = refs[0], refs[1]
    a_refs = refs[2:2 + n_in]
    w_ref = refs[2 + n_in]
    o_ref = refs[3 + n_in]
    y = None
    k0 = 0
    for a_ref in a_refs:
        kw = a_ref.shape[1]
        t = jnp.dot(a_ref[...], w_ref[k0:k0 + kw, :], preferred_element_type=F32)
        y = t if y is None else y + t
        k0 += kw
    o_ref[...] = x_ref[...] + mod_ref[2:3, :] * y


def _out_proj(x, mod3, acts, w):
    s = x.shape[0]
    tm = min(s, 512)
    const = lambda i: (0, 0)
    row = lambda i: (i, 0)
    return pl.pallas_call(
        functools.partial(_out_kernel, n_in=len(acts)),
        grid=(s // tm,),
        in_specs=[pl.BlockSpec((tm, D_MODEL), row), pl.BlockSpec((3, D_MODEL), const)]
                 + [pl.BlockSpec((tm, a.shape[1]), row) for a in acts]
                 + [pl.BlockSpec(w.shape, const)],
        out_specs=pl.BlockSpec((tm, D_MODEL), row),
        out_shape=jax.ShapeDtypeStruct((s, D_MODEL), F32),
        compiler_params=_cparams(("parallel",)),
        name="out_proj",
    )(x, mod3, *acts, w)


def _pad_lanes(v, lane0):
    return jnp.zeros((1, LANES), F32).at[0, lane0:lane0 + v.shape[0]].set(v.astype(F32))


def _swap_halves(w):
    half = w.shape[-1] // 2
    return jnp.concatenate([w[..., half:], w[..., :half]], axis=-1)


def _even_w_in(w):
    qkv, z, beta, a, cq, ckv, kr = jnp.split(
        w, [GDN_CONV_CH, GDN_CONV_CH + GDN_WIDTH, GDN_CONV_CH + GDN_WIDTH + GDN_HEADS,
            GDN_CONV_CH + GDN_WIDTH + 2 * GDN_HEADS, GDN_CONV_CH + GDN_WIDTH + 2 * GDN_HEADS + MLA_Q_RANK,
            GDN_CONV_CH + GDN_WIDTH + 2 * GDN_HEADS + MLA_Q_RANK + MLA_KV_RANK], axis=1)
    pad = jnp.zeros((D_MODEL, LANES - 2 * GDN_HEADS), w.dtype)
    return jnp.concatenate([qkv, z, cq, ckv, kr, _swap_halves(kr), beta, a, pad], axis=1).astype(BF16)


def _even_w_uq(w):
    pe = w[..., MLA_NOPE:]
    return jnp.concatenate([w, _swap_halves(pe)], axis=-1).reshape(MLA_Q_RANK, MLA_HEADS * 2 * LANES).astype(BF16)


def _odd_w_in(w):
    pad = jnp.zeros((D_MODEL, LANES - SSD_HEADS), w.dtype)
    return jnp.concatenate([w, pad], axis=1).astype(BF16)


def kernel(x, c, positions, ada_w, ada_b, norm_g, ffn_w1, ffn_w3, ffn_w2, ev_w_in, gdn_conv_w, gdn_A_log, gdn_dt_bias, gdn_norm_g, mla_q_norm_g, mla_w_uq, mla_kv_norm_g, mla_w_ukv, ev_w_out, ssd_w_in, ssd_conv_w, ssd_conv_b, ssd_A_log, ssd_dt_bias, ssd_D, ssd_norm_g, ssd_w_out, final_g):
    b, s, d = x.shape
    assert b == 1 and d == D_MODEL and s % 256 == 0
    xs = x.reshape(s, d)
    mod = _modulation(c.reshape(d, 1), ada_w, ada_b).reshape(DEPTH, 3, 3, d)
    rope = _rope_table(positions.reshape(s, 1))
    fg = final_g.reshape(1, d)
    w1b, w3b, w2b = ffn_w1.astype(BF16), ffn_w3.astype(BF16), ffn_w2.astype(BF16)
    for l in range(DEPTH):
        last = l == DEPTH - 1
        xs = _ffn(xs, mod[l, 0], norm_g[l, 0].reshape(1, d), w1b, w3b, w2b, l, 0, fg, False)
        g1 = norm_g[l, 1].reshape(1, d)
        if l % 2 == 0:
            e = l // 2
            qkv, z, ba, q, k, v = _even_in(
                xs, mod[l, 1], g1, _even_w_in(ev_w_in[e]), gdn_conv_w[e], rope,
                mla_q_norm_g[e].reshape(1, -1), _even_w_uq(mla_w_uq[e]),
                mla_kv_norm_g[e].reshape(1, -1),
                mla_w_ukv[e].reshape(MLA_KV_RANK, MLA_HEADS * 2 * LANES).astype(BF16))
            o_a = _gdn(qkv, ba, z, _pad_lanes(gdn_A_log[e], GDN_HEADS), _pad_lanes(gdn_dt_bias[e], GDN_HEADS),
                       gdn_norm_g[e].reshape(1, -1))
            o_b = _flash(q, k, v)
            xs = _out_proj(xs, mod[l, 1], [o_a, o_b], ev_w_out[e].astype(BF16))
        else:
            o = l // 2
            z, xbc, dt = _odd_in(xs, mod[l, 1], g1, _odd_w_in(ssd_w_in[o]), ssd_conv_w[o],
                                 ssd_conv_b[o].reshape(1, -1))
            y = _ssd(xbc, dt, z, _pad_lanes(ssd_A_log[o], 0), _pad_lanes(ssd_dt_bias[o], 0),
                     jnp.repeat(ssd_D[o].astype(F32), SSD_HEADDIM).reshape(1, -1), ssd_norm_g[o].reshape(1, -1))
            xs = _out_proj(xs, mod[l, 1], [y], ssd_w_out[o].astype(BF16))
        xs = _ffn(xs, mod[l, 2], norm_g[l, 2].reshape(1, d), w1b, w3b, w2b, l, 1, fg, last)
    return xs.reshape(b, s, d)
```

```python
import functools
import math

import jax
import jax.numpy as jnp
from jax import lax
from jax.experimental import pallas as pl
from jax.experimental.pallas import tpu as pltpu

F32 = jnp.float32
BF16 = jnp.bfloat16

D_MODEL = 1024
DEPTH = 4
CHUNK = 64
NORM_EPS = 1e-6
CONV_K = 4
D_FF = 2816
N_MOD = 9

GDN_HEADS = 4
GDN_DK = 128
GDN_DV = 128
GDN_WIDTH = GDN_HEADS * GDN_DV
GDN_CONV_CH = 2 * GDN_HEADS * GDN_DK + GDN_WIDTH

MLA_HEADS = 4
MLA_Q_RANK = 384
MLA_KV_RANK = 256
MLA_NOPE = 128
MLA_ROPE = 64
MLA_DV = 128
MLA_WIDTH = MLA_HEADS * MLA_DV
MLA_QK = MLA_NOPE + MLA_ROPE
ROPE_THETA = 10000.0

SSD_D_INNER = 2 * D_MODEL
SSD_HEADDIM = 64
SSD_HEADS = SSD_D_INNER // SSD_HEADDIM
SSD_GROUPS = 4
SSD_STATE = 128
SSD_CONV_CH = SSD_D_INNER + 2 * SSD_GROUPS * SSD_STATE
SSD_GROUP_W = SSD_D_INNER // SSD_GROUPS
SSD_HEADS_PER_GROUP = SSD_HEADS // SSD_GROUPS

LANES = 128
SUBLANES = 8
VMEM_LIMIT = 56 * 1024 * 1024
FLASH_T = 512

EV_QKV = 0
EV_Z = GDN_CONV_CH
EV_CQ = EV_Z + GDN_WIDTH
EV_CKV = EV_CQ + MLA_Q_RANK
EV_KR = EV_CKV + MLA_KV_RANK
EV_BA = EV_KR + 2 * MLA_ROPE
EV_IN_P = EV_BA + LANES

OD_Z = 0
OD_XBC = SSD_D_INNER
OD_DT = OD_XBC + SSD_CONV_CH
OD_IN_P = OD_DT + LANES


def _cparams(sem):
    return pltpu.CompilerParams(dimension_semantics=sem, vmem_limit_bytes=VMEM_LIMIT)


def _bdot(a, b):
    return jnp.dot(a.astype(BF16), b.astype(BF16), preferred_element_type=F32)


def _bdot_nt(a, b):
    return lax.dot_general(a.astype(BF16), b.astype(BF16), (((1,), (1,)), ((), ())),
                           preferred_element_type=F32)


def _mask_dot(mask, x):
    m = jnp.where(mask, 1.0, 0.0).astype(BF16)
    out = None
    r = x
    for _ in range(3):
        part = r.astype(BF16)
        r = r - part.astype(F32)
        t = jnp.dot(m, part, preferred_element_type=F32)
        out = t if out is None else out + t
    return out


def _silu(x):
    hx = 0.5 * x
    return hx + hx * jnp.tanh(hx)


def _softplus(x):
    return jnp.maximum(x, 0.0) + jnp.log1p(jnp.exp(-jnp.abs(x)))


def _adaln(x, g, shift, scale):
    ms = jnp.mean(x * x, axis=-1, keepdims=True)
    y = x * lax.rsqrt(ms + NORM_EPS) * g
    return y * (1.0 + scale) + shift


def _mod_kernel(c_ref, w_ref, b_ref, o_ref):
    ca = _silu(c_ref[...])
    cb = jnp.broadcast_to(ca, (D_MODEL, LANES))
    tn = w_ref.shape[2]
    for t in range(tn // LANES):
        sl = slice(t * LANES, (t + 1) * LANES)
        o_ref[0, :, sl] = jnp.sum(w_ref[0, :, sl] * cb, axis=0, keepdims=True) + b_ref[0, :, sl]


def _modulation(c_col, ada_w, ada_b):
    n = N_MOD * D_MODEL
    tn = n // 4
    return pl.pallas_call(
        _mod_kernel,
        grid=(DEPTH, n // tn),
        in_specs=[pl.BlockSpec((D_MODEL, 1), lambda l, j: (0, 0)),
                  pl.BlockSpec((1, D_MODEL, tn), lambda l, j: (l, 0, j)),
                  pl.BlockSpec((1, 1, tn), lambda l, j: (l, 0, j))],
        out_specs=pl.BlockSpec((1, 1, tn), lambda l, j: (l, 0, j)),
        out_shape=jax.ShapeDtypeStruct((DEPTH, 1, n), F32),
        compiler_params=_cparams(("parallel", "parallel")),
        name="modulation",
    )(c_col, ada_w, ada_b.reshape(DEPTH, 1, n))


def _rope_kernel(pos_ref, freq_ref, o_ref):
    ang = pos_ref[...].astype(F32) * freq_ref[...]
    lane = lax.broadcasted_iota(jnp.int32, ang.shape, 1)
    sgn = jnp.where(lane < 3 * (MLA_ROPE // 2), -1.0, 1.0)
    o_ref[...] = jnp.where(lane < MLA_ROPE, jnp.cos(ang), sgn * jnp.sin(ang))


def _rope_table(pos_col):
    s = pos_col.shape[0]
    tm = min(s, 2048)
    half = MLA_ROPE // 2
    inv_freq = ROPE_THETA ** (-jnp.arange(half, dtype=F32) / half)
    freq = jnp.tile(inv_freq, LANES // half).reshape(1, LANES)
    return pl.pallas_call(
        _rope_kernel,
        grid=(s // tm,),
        in_specs=[pl.BlockSpec((tm, 1), lambda i: (i, 0)),
                  pl.BlockSpec((1, LANES), lambda i: (0, 0))],
        out_specs=pl.BlockSpec((tm, LANES), lambda i: (i, 0)),
        out_shape=jax.ShapeDtypeStruct((s, LANES), F32),
        compiler_params=_cparams(("parallel",)),
        name="rope_table",
    )(pos_col, freq)


def _ffn_kernel(*refs, tf, final, n_mix):
    x_ref, mod_ref, g_ref, w1_ref, w3_ref, w2_ref, fg_ref = refs[0:7]
    n_extra = n_mix + 2 if n_mix else 0
    mix_refs = refs[7:7 + n_extra]
    o_ref, a_ref = refs[7 + n_extra:]
    x = x_ref[...]
    if n_mix:
        mmod_ref, w_out_ref = mix_refs[0], mix_refs[-1]
        y = None
        k0 = 0
        for act_ref in mix_refs[1:-1]:
            kw = act_ref.shape[1]
            t = jnp.dot(act_ref[...], w_out_ref[k0:k0 + kw, :], preferred_element_type=F32)
            y = t if y is None else y + t
            k0 += kw
        x = x + mmod_ref[2:3, :] * y
    h = _adaln(x, g_ref[...], mod_ref[0:1, :], mod_ref[1:2, :]).astype(BF16)
    for j in range(D_FF // tf):
        sl = slice(j * tf, (j + 1) * tf)
        h1 = jnp.dot(h, w1_ref[:, sl], preferred_element_type=F32)
        h3 = jnp.dot(h, w3_ref[:, sl], preferred_element_type=F32)
        a_ref[:, sl] = (_silu(h1) * h3).astype(BF16)
    y = jnp.dot(a_ref[...], w2_ref[...], preferred_element_type=F32)
    xn = x + (0.5 * mod_ref[2:3, :]) * y
    if final:
        ms = jnp.mean(xn * xn, axis=-1, keepdims=True)
        xn = xn * lax.rsqrt(ms + NORM_EPS) * fg_ref[...]
    o_ref[...] = xn


def _ffn(x, mod3, g, w1, w3, w2, layer, which, final_g, final, mix=None):
    s = x.shape[0]
    tm = min(s, 512)
    const = lambda i: (0, 0)
    row = lambda i: (i, 0)
    wsel = lambda i: (layer, which, 0, 0)
    in_specs = [pl.BlockSpec((tm, D_MODEL), row),
                pl.BlockSpec((3, D_MODEL), const),
                pl.BlockSpec((1, D_MODEL), const),
                pl.BlockSpec((None, None, D_MODEL, D_FF), wsel, pipeline_mode=pl.Buffered(1)),
                pl.BlockSpec((None, None, D_MODEL, D_FF), wsel, pipeline_mode=pl.Buffered(1)),
                pl.BlockSpec((None, None, D_FF, D_MODEL), wsel, pipeline_mode=pl.Buffered(1)),
                pl.BlockSpec((1, D_MODEL), const)]
    args = [x, mod3, g, w1, w3, w2, final_g]
    n_mix = 0
    if mix is not None:
        mmod, acts, w_out = mix
        n_mix = len(acts)
        in_specs += ([pl.BlockSpec((3, D_MODEL), const)]
                     + [pl.BlockSpec((tm, a.shape[1]), row) for a in acts]
                     + [pl.BlockSpec(w_out.shape, const, pipeline_mode=pl.Buffered(1))])
        args += [mmod, *acts, w_out]
    return pl.pallas_call(
        functools.partial(_ffn_kernel, tf=256, final=final, n_mix=n_mix),
        grid=(s // tm,),
        in_specs=in_specs,
        out_specs=pl.BlockSpec((tm, D_MODEL), row),
        out_shape=jax.ShapeDtypeStruct((s, D_MODEL), F32),
        scratch_shapes=[pltpu.VMEM((tm, D_FF), BF16)],
        compiler_params=_cparams(("parallel",)),
        name="ffn_final" if final else ("mix_ffn" if n_mix else "ffn"),
    )(*args)


def _causal_conv(p, cw, ext_ref, halo_ref, col0):
    tm, w = p.shape
    cs = slice(col0, col0 + w)
    ext_ref[0:SUBLANES, 0:w] = halo_ref[:, cs]
    ext_ref[SUBLANES:SUBLANES + tm, 0:w] = p
    halo_ref[:, cs] = p[tm - SUBLANES:, :]
    y = p * cw[CONV_K - 1:CONV_K, :]
    for j in range(CONV_K - 1):
        off = SUBLANES - (CONV_K - 1) + j
        y = y + ext_ref[off:off + tm, 0:w] * cw[j:j + 1, :]
    return y


def _even_in_kernel(x_ref, mod_ref, g_ref, w_ref, cw_ref, rope_ref, qg_ref, wuq_ref, kvg_ref, wukv_ref,
                    qkv_ref, z_ref, ba_ref, q_ref, k_ref, v_ref, ext_ref, halo_ref):
    @pl.when(pl.program_id(0) == 0)
    def _():
        halo_ref[...] = jnp.zeros_like(halo_ref)

    x = x_ref[...]
    h = _adaln(x, g_ref[...], mod_ref[0:1, :], mod_ref[1:2, :]).astype(BF16)
    proj = lambda c0, w: jnp.dot(h, w_ref[:, c0:c0 + w], preferred_element_type=F32)

    rope = rope_ref[...]
    scale = MLA_QK ** -0.5 * math.log2(math.e)

    def rot(xx):
        yy = xx * rope
        return yy + pltpu.roll(yy, MLA_ROPE, 1)

    def rms(v, g):
        return (v * lax.rsqrt(jnp.mean(v * v, axis=-1, keepdims=True) + NORM_EPS) * g).astype(BF16)

    def gdn_part(part, p):
        c0 = part * GDN_WIDTH
        y = _silu(_causal_conv(p, cw_ref[:, c0:c0 + GDN_WIDTH], ext_ref, halo_ref, c0))
        if part == 2:
            qkv_ref[:, c0:c0 + GDN_WIDTH] = y
            return
        post = GDN_DK ** -0.5 if part == 0 else 1.0
        for hd in range(GDN_HEADS):
            yh = y[:, hd * GDN_DK:(hd + 1) * GDN_DK]
            nrm = lax.rsqrt(jnp.sum(yh * yh, axis=-1, keepdims=True) + NORM_EPS)
            qkv_ref[:, c0 + hd * GDN_DK:c0 + (hd + 1) * GDN_DK] = yh * nrm * post

    def mla_q(qf):
        for hd in range(MLA_HEADS):
            b0 = hd * 2 * LANES
            q_ref[hd, :, 0:MLA_NOPE] = (qf[:, b0:b0 + MLA_NOPE] * scale).astype(BF16)
            qpe = rot(qf[:, b0 + LANES:b0 + 2 * LANES]) * scale
            q_ref[hd, :, MLA_NOPE:MLA_QK] = qpe[:, 0:MLA_ROPE].astype(BF16)

    def mla_kv(kvf, kpe):
        for hd in range(MLA_HEADS):
            b0 = hd * 2 * LANES
            k_ref[hd, :, 0:MLA_NOPE] = kvf[:, b0:b0 + MLA_NOPE].astype(BF16)
            k_ref[hd, :, MLA_NOPE:MLA_QK] = kpe
            v_ref[hd] = kvf[:, b0 + LANES:b0 + 2 * LANES].astype(BF16)

    cq, ckv, kr = proj(EV_CQ, MLA_Q_RANK), proj(EV_CKV, MLA_KV_RANK), proj(EV_KR, 2 * MLA_ROPE)
    p0 = proj(0, GDN_WIDTH)
    cqn, ckvn = rms(cq, qg_ref[...]), rms(ckv, kvg_ref[...])
    kpe = rot(kr)[:, 0:MLA_ROPE].astype(BF16)
    qf = jnp.dot(cqn, wuq_ref[...], preferred_element_type=F32)
    gdn_part(0, p0)
    p1 = proj(GDN_WIDTH, GDN_WIDTH)
    mla_q(qf)
    kvf = jnp.dot(ckvn, wukv_ref[...], preferred_element_type=F32)
    gdn_part(1, p1)
    p2 = proj(2 * GDN_WIDTH, GDN_WIDTH)
    mla_kv(kvf, kpe)
    zb = proj(EV_Z, GDN_WIDTH), proj(EV_BA, LANES)
    gdn_part(2, p2)
    z_ref[...], ba_ref[...] = zb


def _even_in(x, mod3, g, w_in_p, conv_w, rope, qg, wuq_p, kvg, wukv_p):
    s = x.shape[0]
    tm = min(s, 512)
    const = lambda i: (0, 0)
    row = lambda i: (i, 0)
    hrow = lambda i: (0, i, 0)
    return pl.pallas_call(
        _even_in_kernel,
        grid=(s // tm,),
        in_specs=[pl.BlockSpec((tm, D_MODEL), row),
                  pl.BlockSpec((3, D_MODEL), const),
                  pl.BlockSpec((1, D_MODEL), const),
                  pl.BlockSpec((D_MODEL, EV_IN_P), const),
                  pl.BlockSpec((CONV_K, GDN_CONV_CH), const),
                  pl.BlockSpec((tm, LANES), row),
                  pl.BlockSpec((1, MLA_Q_RANK), const),
                  pl.BlockSpec((MLA_Q_RANK, MLA_HEADS * 2 * LANES), const),
                  pl.BlockSpec((1, MLA_KV_RANK), const),
                  pl.BlockSpec((MLA_KV_RANK, MLA_HEADS * 2 * LANES), const)],
        out_specs=[pl.BlockSpec((tm, GDN_CONV_CH), row),
                   pl.BlockSpec((tm, GDN_WIDTH), row),
                   pl.BlockSpec((tm, LANES), row),
                   pl.BlockSpec((MLA_HEADS, tm, MLA_QK), hrow),
                   pl.BlockSpec((MLA_HEADS, tm, MLA_QK), hrow),
                   pl.BlockSpec((MLA_HEADS, tm, MLA_DV), hrow)],
        out_shape=[jax.ShapeDtypeStruct((s, GDN_CONV_CH), F32),
                   jax.ShapeDtypeStruct((s, GDN_WIDTH), F32),
                   jax.ShapeDtypeStruct((s, LANES), F32),
                   jax.ShapeDtypeStruct((MLA_HEADS, s, MLA_QK), BF16),
                   jax.ShapeDtypeStruct((MLA_HEADS, s, MLA_QK), BF16),
                   jax.ShapeDtypeStruct((MLA_HEADS, s, MLA_DV), BF16)],
        scratch_shapes=[pltpu.VMEM((tm + SUBLANES, GDN_WIDTH), F32),
                        pltpu.VMEM((SUBLANES, GDN_CONV_CH), F32)],
        compiler_params=_cparams(("arbitrary",)),
        name="even_in",
    )(x, mod3, g, w_in_p, conv_w, rope, qg, wuq_p, kvg, wukv_p)


GDN_SB = 256
GDN_TB = 512


def _gdn_kernel(qkv_ref, ba_ref, z_ref, alog_ref, dtb_ref, gn_ref, o_ref, s_ref):
    tb = GDN_SB
    nch = tb // CHUNK
    subs = range(GDN_TB // GDN_SB)

    @pl.when(pl.program_id(0) == 0)
    def _():
        s_ref[...] = jnp.zeros_like(s_ref)

    lane = lax.broadcasted_iota(jnp.int32, (tb, LANES), 1)
    ri = lax.broadcasted_iota(jnp.int32, (tb, tb), 0)
    ci = lax.broadcasted_iota(jnp.int32, (tb, tb), 1)
    same = (ri // CHUNK) == (ci // CHUNK)
    beta_all, gc_all, gc_t = [], [], []
    for b in subs:
        ba = ba_ref[b * tb:(b + 1) * tb, :]
        beta_all.append(jax.nn.sigmoid(ba))
        g_all = jnp.where((lane >= GDN_HEADS) & (lane < 2 * GDN_HEADS),
                          -jnp.exp(alog_ref[...]) * _softplus(ba + dtb_ref[...]), 0.0)
        gc_all.append(_mask_dot(same & (ci <= ri), g_all))
        gc_t.append(gc_all[b].T)
    gn = gn_ref[...]
    bd_mask = jnp.where(same, 1.0, 0.0).astype(BF16)

    wi = lax.broadcasted_iota(jnp.int32, (CHUNK, tb), 0)
    wl = lax.broadcasted_iota(jnp.int32, (CHUNK, tb), 1)
    wj, wc = wl % CHUNK, wl // CHUNK
    w_lower, w_strict = wj <= wi, wj < wi
    w_eye = jnp.where(wj == wi, 1.0, 0.0)

    def fold(full):
        out = full[0:CHUNK]
        for c in range(1, nch):
            out = jnp.where(wc == c, full[c * CHUNK:(c + 1) * CHUNK], out)
        return out

    def block_diag(xb):
        return jnp.concatenate([xb] * nch, axis=0) * bd_mask

    def split(x):
        hi = x.astype(BF16)
        return hi, (x - hi.astype(F32)).astype(BF16)

    def wide_dot3(xs, b):
        bh, bl = split(b)
        parts = [split(x) for x in xs]
        r = jnp.dot(jnp.concatenate([t for hl in parts for t in hl], axis=0), block_diag(bh),
                    preferred_element_type=F32)
        r2 = jnp.dot(jnp.concatenate([hl[0] for hl in parts], axis=0), block_diag(bl),
                     preferred_element_type=F32)
        return [r[2 * i * CHUNK:(2 * i + 1) * CHUNK] + r[(2 * i + 1) * CHUNK:(2 * i + 2) * CHUNK]
                + r2[i * CHUNK:(i + 1) * CHUNK] for i in range(len(xs))]

    heads = range(GDN_HEADS)
    units = [(b, hd) for b in subs for hd in heads]
    hv = []
    for b, hd in units:
        rows = slice(b * tb, (b + 1) * tb)
        gcol = gc_all[b][:, GDN_HEADS + hd:GDN_HEADS + hd + 1]
        grow = gc_t[b][GDN_HEADS + hd:GDN_HEADS + hd + 1, :]
        gcol_w = fold(jnp.broadcast_to(gcol, (tb, tb)))
        decay = jnp.where(w_lower, jnp.exp(jnp.where(w_lower, gcol_w - grow, 0.0)), 0.0)
        beta = beta_all[b][:, hd:hd + 1]
        q = qkv_ref[rows, hd * GDN_DK:(hd + 1) * GDN_DK]
        k = qkv_ref[rows, GDN_WIDTH + hd * GDN_DK:GDN_WIDTH + (hd + 1) * GDN_DK]
        v = qkv_ref[rows, 2 * GDN_WIDTH + hd * GDN_DV:2 * GDN_WIDTH + (hd + 1) * GDN_DV]
        kb = k * beta
        egc = jnp.exp(gcol)
        gl = jnp.concatenate([jnp.broadcast_to(gcol[(c + 1) * CHUNK - 1:(c + 1) * CHUNK, :], (CHUNK, 1))
                              for c in range(nch)], axis=0)
        hv.append(dict(decay=decay, q=q, k=k, kb=kb, qs=q * egc, egl=jnp.exp(gl),
                       rhs=jnp.concatenate([v * beta, kb * egc], axis=1).astype(BF16),
                       kend_t=(k * jnp.exp(gl - gcol)).T))

    pw = [jnp.where(w_strict, -(fold(_bdot_nt(h["kb"], h["k"])) * h["decay"]), 0.0) for h in hv]
    tw = [w_eye + p for p in pw]
    pw = [wide_dot3([p], p)[0] for p in pw]
    for _ in range(4):
        res = [wide_dot3([t, p], p) for t, p in zip(tw, pw)]
        tw = [t + r[0] for t, r in zip(tw, res)]
        pw = [r[1] for r in res]
    tw = [t + wide_dot3([t], p)[0] for t, p in zip(tw, pw)]
    pre = {}
    for i, unit in enumerate(units):
        h = hv[i]
        sol = jnp.dot(block_diag(tw[i].astype(BF16)), h["rhs"], preferred_element_type=F32)
        attn = fold(_bdot_nt(h["q"], h["k"])) * h["decay"]
        pre[unit] = (sol[:, 0:GDN_DV], sol[:, GDN_DV:], attn, h["qs"], h["kend_t"], h["egl"])

    col_chunk = lax.broadcasted_iota(jnp.int32, (GDN_DK, tb), 1) // CHUNK
    states = [s_ref[hd] for hd in heads]
    for b in subs:
        for c in range(nch):
            rs = slice(c * CHUNK, (c + 1) * CHUNK)
            pc = [pre[(b, hd)] for hd in heads]
            wss = [_bdot(jnp.concatenate([pc[hd][1][rs], pc[hd][3][rs]], axis=0), states[hd]) for hd in heads]
            v_full = []
            for hd in heads:
                parts = [jnp.zeros((CHUNK, GDN_DV), F32)] * nch
                parts[c] = pc[hd][0][rs] - wss[hd][0:CHUNK]
                v_full.append(jnp.concatenate(parts, axis=0))
            states = [states[hd] * pc[hd][5][c * CHUNK:c * CHUNK + 1, :]
                      + _bdot(jnp.where(col_chunk == c, pc[hd][4], 0.0), v_full[hd]) for hd in heads]
            out_rows = slice(b * tb + c * CHUNK, b * tb + (c + 1) * CHUNK)
            for hd in heads:
                o = wss[hd][CHUNK:] + _bdot(jnp.where(wc == c, pc[hd][2], 0.0), v_full[hd])
                ms = jnp.mean(o * o, axis=-1, keepdims=True)
                zz = z_ref[out_rows, hd * GDN_DV:(hd + 1) * GDN_DV]
                o_ref[out_rows, hd * GDN_DV:(hd + 1) * GDN_DV] = (
                    o * lax.rsqrt(ms + NORM_EPS) * gn * _silu(zz)).astype(BF16)
    for hd in heads:
        s_ref[hd] = states[hd]


def _gdn(qkv, ba, z, alog_row, dtb_row, gn):
    s = qkv.shape[0]
    tb = GDN_TB
    const = lambda i: (0, 0)
    row = lambda i: (i, 0)
    return pl.pallas_call(
        _gdn_kernel,
        grid=(s // tb,),
        in_specs=[pl.BlockSpec((tb, GDN_CONV_CH), row),
                  pl.BlockSpec((tb, LANES), row),
                  pl.BlockSpec((tb, GDN_WIDTH), row),
                  pl.BlockSpec((1, LANES), const),
                  pl.BlockSpec((1, LANES), const),
                  pl.BlockSpec((1, GDN_DV), const)],
        out_specs=pl.BlockSpec((tb, GDN_WIDTH), row),
        out_shape=jax.ShapeDtypeStruct((s, GDN_WIDTH), BF16),
        scratch_shapes=[pltpu.VMEM((GDN_HEADS, GDN_DK, GDN_DV), F32)],
        compiler_params=_cparams(("arbitrary",)),
        name="gdn",
    )(qkv, ba, z, alog_row, dtb_row, gn)


def _flash_kernel(q_ref, k_ref, v_ref, o_ref, m_ref, l_ref, acc_ref):
    t = FLASH_T
    qi = pl.program_id(1)
    nrep = t // LANES
    m_ref[...] = jnp.full_like(m_ref, -jnp.inf)
    l_ref[...] = jnp.zeros_like(l_ref)
    acc_ref[...] = jnp.zeros_like(acc_ref)

    def attend(kb, halves):
        start = pl.multiple_of(kb * t, t)
        k = k_ref[0, pl.ds(start, t), :]
        v = v_ref[0, pl.ds(start, t), :]
        scores = []
        for half, diagonal in halves:
            q = q_ref[0, half * t:(half + 1) * t, :]
            s = lax.dot_general(q, k, (((1,), (1,)), ((), ())), preferred_element_type=F32)
            if diagonal:
                qc = lax.broadcasted_iota(jnp.int32, (t, t), 0) // CHUNK
                kc = lax.broadcasted_iota(jnp.int32, (t, t), 1) // CHUNK
                s = jnp.where(kc <= qc, s, -jnp.inf)
            scores.append(s)
        probs = []
        for (half, _), s in zip(halves, scores):
            m_old = m_ref[half]
            m_new = jnp.maximum(m_old, jnp.max(s, axis=-1, keepdims=True))
            alpha = jnp.exp2(m_old - m_new)
            p = jnp.exp2(s - jnp.concatenate([m_new] * nrep, axis=1))
            psum = p[:, 0:LANES]
            for r in range(1, nrep):
                psum = psum + p[:, r * LANES:(r + 1) * LANES]
            l_ref[half] = l_ref[half] * alpha + psum
            m_ref[half] = m_new
            probs.append((alpha, p.astype(BF16)))
        for (half, _), (alpha, p) in zip(halves, probs):
            acc_ref[half] = acc_ref[half] * alpha + jnp.dot(p, v, preferred_element_type=F32)

    def body(i, carry):
        for kb in range(4):
            attend(4 * i + kb, [(0, False), (1, False)])
        return carry

    lax.fori_loop(0, qi // 2, body, 0)

    @pl.when(qi % 2 == 1)
    def _():
        for kb in (2 * qi - 2, 2 * qi - 1):
            attend(kb, [(0, False), (1, False)])

    attend(2 * qi, [(0, True), (1, False)])
    attend(2 * qi + 1, [(1, True)])
    for half in range(2):
        l = jnp.sum(l_ref[half], axis=-1, keepdims=True)
        o_ref[half * t:(half + 1) * t, :] = (acc_ref[half] / l).astype(o_ref.dtype)


def _flash(q, k, v):
    nh, s, _ = q.shape
    t = FLASH_T
    assert s % (2 * t) == 0
    return pl.pallas_call(
        _flash_kernel,
        grid=(nh, s // (2 * t)),
        in_specs=[pl.BlockSpec((1, 2 * t, MLA_QK), lambda h, qi: (h, qi, 0)),
                  pl.BlockSpec((1, s, MLA_QK), lambda h, qi: (h, 0, 0)),
                  pl.BlockSpec((1, s, MLA_DV), lambda h, qi: (h, 0, 0))],
        out_specs=pl.BlockSpec((2 * t, MLA_DV), lambda h, qi: (qi, h)),
        out_shape=jax.ShapeDtypeStruct((s, MLA_WIDTH), BF16),
        scratch_shapes=[pltpu.VMEM((2, t, LANES), F32), pltpu.VMEM((2, t, LANES), F32),
                        pltpu.VMEM((2, t, MLA_DV), F32)],
        compiler_params=_cparams(("parallel", "arbitrary")),
        name="mla_flash",
    )(q, k, v)


def _odd_in_kernel(x_ref, mod_ref, g_ref, w_ref, cw_ref, cb_ref, z_ref, xbc_ref, dt_ref, ext_ref, halo_ref, *, tc):
    @pl.when(pl.program_id(0) == 0)
    def _():
        halo_ref[...] = jnp.zeros_like(halo_ref)

    x = x_ref[...]
    h = _adaln(x, g_ref[...], mod_ref[0:1, :], mod_ref[1:2, :]).astype(BF16)
    proj = lambda c0, w: jnp.dot(h, w_ref[:, c0:c0 + w], preferred_element_type=F32)

    def finish(kind, c0, p):
        if kind == "z":
            z_ref[:, c0:c0 + tc] = p
        elif kind == "dt":
            dt_ref[...] = p
        else:
            y = _causal_conv(p, cw_ref[:, c0:c0 + tc], ext_ref, halo_ref, c0)
            xbc_ref[:, c0:c0 + tc] = _silu(y + cb_ref[:, c0:c0 + tc])

    nz, nx = SSD_D_INNER // tc, SSD_CONV_CH // tc
    tasks = []
    for j in range(max(nz, nx)):
        if j < nx:
            tasks.append(("xbc", j * tc, OD_XBC + j * tc, tc))
        if j < nz:
            tasks.append(("z", j * tc, OD_Z + j * tc, tc))
    tasks.append(("dt", 0, OD_DT, LANES))
    pending = None
    for kind, c0, wc0, width in tasks:
        p = proj(wc0, width)
        if pending is not None:
            finish(*pending)
        pending = (kind, c0, p)
    finish(*pending)


def _odd_in(x, mod3, g, w_in_p, conv_w, conv_b):
    s = x.shape[0]
    tm = min(s, 512)
    tc = 512
    const = lambda i: (0, 0)
    row = lambda i: (i, 0)
    return pl.pallas_call(
        functools.partial(_odd_in_kernel, tc=tc),
        grid=(s // tm,),
        in_specs=[pl.BlockSpec((tm, D_MODEL), row),
                  pl.BlockSpec((3, D_MODEL), const),
                  pl.BlockSpec((1, D_MODEL), const),
                  pl.BlockSpec((D_MODEL, OD_IN_P), const),
                  pl.BlockSpec((CONV_K, SSD_CONV_CH), const),
                  pl.BlockSpec((1, SSD_CONV_CH), const)],
        out_specs=[pl.BlockSpec((tm, SSD_D_INNER), row),
                   pl.BlockSpec((tm, SSD_CONV_CH), row),
                   pl.BlockSpec((tm, LANES), row)],
        out_shape=[jax.ShapeDtypeStruct((s, SSD_D_INNER), F32),
                   jax.ShapeDtypeStruct((s, SSD_CONV_CH), F32),
                   jax.ShapeDtypeStruct((s, LANES), F32)],
        scratch_shapes=[pltpu.VMEM((tm + SUBLANES, tc), F32),
                        pltpu.VMEM((SUBLANES, SSD_CONV_CH), F32)],
        compiler_params=_cparams(("arbitrary",)),
        name="odd_in",
    )(x, mod3, g, w_in_p, conv_w, conv_b)


SSD_TB = 256
SSD_CHUNK = 128


def _ssd_kernel(xbc_ref, dt_ref, z_ref, alog_ref, dtb_ref, dskip_ref, ng_ref, o_ref, hs_ref):
    tb, lc = SSD_TB, SSD_CHUNK

    @pl.when(pl.program_id(0) == 0)
    def _():
        hs_ref[...] = jnp.zeros_like(hs_ref)

    lane = lax.broadcasted_iota(jnp.int32, (tb, LANES), 1)
    dt = jnp.where(lane < SSD_HEADS, _softplus(dt_ref[...] + dtb_ref[...]), 0.0)
    da = dt * (-jnp.exp(alog_ref[...]))
    ri = lax.broadcasted_iota(jnp.int32, (tb, tb), 0)
    ci = lax.broadcasted_iota(jnp.int32, (tb, tb), 1)
    acs = _mask_dot(((ri // lc) == (ci // lc)) & (ci <= ri), da)
    acs = acs * math.log2(math.e)
    acs_t = acs.T
    ldt_t = jnp.log2(jnp.where(lane < SSD_HEADS, dt, 1.0)).T
    left = lax.broadcasted_iota(jnp.int32, (lc, LANES), 1) < SSD_HEADDIM
    halves = (left, jnp.logical_not(left))
    causal = jnp.where(lax.broadcasted_iota(jnp.int32, (lc, lc), 1) <= lax.broadcasted_iota(jnp.int32, (lc, lc), 0),
                       0.0, -1e30)
    groups = range(SSD_GROUPS)
    pairs = range(SSD_HEADS_PER_GROUP // 2)

    def gcols(g):
        return slice(g * SSD_GROUP_W, (g + 1) * SSD_GROUP_W)

    def pcols(pr):
        return slice(pr * LANES, (pr + 1) * LANES)

    states = [[hs_ref[g, :, pcols(pr)] for pr in pairs] for g in groups]
    for sc in range(tb // lc):
        rs = slice(sc * lc, (sc + 1) * lc)
        acs_c, acs_tc, ldt_tc = acs[rs], acs_t[:, rs], ldt_t[:, rs]
        a_last = acs_c[lc - 1:lc, :]
        c_decay = jnp.exp2(a_last)
        xs = [xbc_ref[rs, gcols(g)] for g in groups]
        bg_t = [xbc_ref[rs, SSD_D_INNER + g * SSD_STATE:SSD_D_INNER + (g + 1) * SSD_STATE].T for g in groups]
        cg = [xbc_ref[rs, SSD_D_INNER + (SSD_GROUPS + g) * SSD_STATE:SSD_D_INNER + (SSD_GROUPS + g + 1) * SSD_STATE]
              for g in groups]
        cb = [lax.dot_general(cg[g].astype(BF16), bg_t[g].astype(BF16), (((1,), (0,)), ((), ())),
                              preferred_element_type=F32) for g in groups]
        ys = [[None] * len(pairs) for _ in groups]
        new_states = [[None] * len(pairs) for _ in groups]
        for pr in pairs:
            for g in groups:
                x_pair = xs[g][:, pcols(pr)]
                st = states[g][pr]
                acc = None
                upd = None
                for half in range(2):
                    hd = g * SSD_HEADS_PER_GROUP + 2 * pr + half
                    colb = jnp.broadcast_to(acs_c[:, hd:hd + 1], (lc, LANES))
                    lrow = ldt_tc[hd:hd + 1, :] - acs_tc[hd:hd + 1, :]
                    m1 = cb[g] * jnp.exp2(colb + (lrow + causal))
                    c1 = cg[g] * jnp.exp2(colb)
                    xh = jnp.where(halves[half], x_pair, 0.0).astype(BF16)
                    sh = jnp.where(halves[half], st, 0.0).astype(BF16)
                    t = jnp.dot(jnp.concatenate([m1, c1], axis=1).astype(BF16), jnp.concatenate([xh, sh], axis=0),
                                preferred_element_type=F32)
                    acc = t if acc is None else acc + t
                    wrow = jnp.exp2(a_last[:, hd:hd + 1] + lrow)
                    u = jnp.dot((bg_t[g] * wrow).astype(BF16), xh, preferred_element_type=F32)
                    upd = u if upd is None else upd + u
                ys[g][pr] = acc
                h0 = g * SSD_HEADS_PER_GROUP + 2 * pr
                cdec = jnp.where(left[0:1], c_decay[:, h0:h0 + 1], c_decay[:, h0 + 1:h0 + 2])
                new_states[g][pr] = st * cdec + upd
        states = new_states
        for g in groups:
            y = jnp.concatenate(ys[g], axis=1) + dskip_ref[:, gcols(g)] * xs[g]
            y = y * _silu(z_ref[rs, gcols(g)])
            ms = jnp.mean(y * y, axis=-1, keepdims=True)
            o_ref[rs, gcols(g)] = (y * lax.rsqrt(ms + NORM_EPS) * ng_ref[:, gcols(g)]).astype(BF16)
    for g in groups:
        for pr in pairs:
            hs_ref[g, :, pcols(pr)] = states[g][pr]


def _ssd(xbc, dt, z, alog_row, dtb_row, dskip_row, ng):
    s = xbc.shape[0]
    tb = SSD_TB
    const = lambda i: (0, 0)
    row = lambda i: (i, 0)
    return pl.pallas_call(
        _ssd_kernel,
        grid=(s // tb,),
        in_specs=[pl.BlockSpec((tb, SSD_CONV_CH), row),
                  pl.BlockSpec((tb, LANES), row),
                  pl.BlockSpec((tb, SSD_D_INNER), row),
                  pl.BlockSpec((1, LANES), const),
                  pl.BlockSpec((1, LANES), const),
                  pl.BlockSpec((1, SSD_D_INNER), const),
                  pl.BlockSpec((1, SSD_D_INNER), const)],
        out_specs=pl.BlockSpec((tb, SSD_D_INNER), row),
        out_shape=jax.ShapeDtypeStruct((s, SSD_D_INNER), BF16),
        scratch_shapes=[pltpu.VMEM((SSD_GROUPS, SSD_STATE, SSD_GROUP_W), F32)],
        compiler_params=_cparams(("arbitrary",)),
        name="ssd",
    )(xbc, dt, z, alog_row, dtb_row, dskip_row, ng)


def _pad_lanes(v, lane0):
    return jnp.zeros((1, LANES), F32).at[0, lane0:lane0 + v.shape[0]].set(v.astype(F32))


def _swap_halves(w):
    half = w.shape[-1] // 2
    return jnp.concatenate([w[..., half:], w[..., :half]], axis=-1)


def _even_w_in(w):
    qkv, z, beta, a, cq, ckv, kr = jnp.split(
        w, [GDN_CONV_CH, GDN_CONV_CH + GDN_WIDTH, GDN_CONV_CH + GDN_WIDTH + GDN_HEADS,
            GDN_CONV_CH + GDN_WIDTH + 2 * GDN_HEADS, GDN_CONV_CH + GDN_WIDTH + 2 * GDN_HEADS + MLA_Q_RANK,
            GDN_CONV_CH + GDN_WIDTH + 2 * GDN_HEADS + MLA_Q_RANK + MLA_KV_RANK], axis=1)
    pad = jnp.zeros((D_MODEL, LANES - 2 * GDN_HEADS), w.dtype)
    return jnp.concatenate([qkv, z, cq, ckv, kr, _swap_halves(kr), beta, a, pad], axis=1).astype(BF16)


def _even_w_uq(w):
    pe = w[..., MLA_NOPE:]
    return jnp.concatenate([w, _swap_halves(pe)], axis=-1).reshape(MLA_Q_RANK, MLA_HEADS * 2 * LANES).astype(BF16)


def _odd_w_in(w):
    pad = jnp.zeros((D_MODEL, LANES - SSD_HEADS), w.dtype)
    return jnp.concatenate([w, pad], axis=1).astype(BF16)


def kernel(x, c, positions, ada_w, ada_b, norm_g, ffn_w1, ffn_w3, ffn_w2, ev_w_in, gdn_conv_w, gdn_A_log, gdn_dt_bias, gdn_norm_g, mla_q_norm_g, mla_w_uq, mla_kv_norm_g, mla_w_ukv, ev_w_out, ssd_w_in, ssd_conv_w, ssd_conv_b, ssd_A_log, ssd_dt_bias, ssd_D, ssd_norm_g, ssd_w_out, final_g):
    b, s, d = x.shape
    assert b == 1 and d == D_MODEL and s % (2 * FLASH_T) == 0
    xs = x.reshape(s, d)
    mod = _modulation(c.reshape(d, 1), ada_w, ada_b).reshape(DEPTH, 3, 3, d)
    rope = _rope_table(positions.reshape(s, 1))
    fg = final_g.reshape(1, d)
    w1b, w3b, w2b = ffn_w1.astype(BF16), ffn_w3.astype(BF16), ffn_w2.astype(BF16)
    for l in range(DEPTH):
        last = l == DEPTH - 1
        xs = _ffn(xs, mod[l, 0], norm_g[l, 0].reshape(1, d), w1b, w3b, w2b, l, 0, fg, False)
        g1 = norm_g[l, 1].reshape(1, d)
        if l % 2 == 0:
            e = l // 2
            qkv, z, ba, q, k, v = _even_in(
                xs, mod[l, 1], g1, _even_w_in(ev_w_in[e]), gdn_conv_w[e], rope,
                mla_q_norm_g[e].reshape(1, -1), _even_w_uq(mla_w_uq[e]),
                mla_kv_norm_g[e].reshape(1, -1),
                mla_w_ukv[e].reshape(MLA_KV_RANK, MLA_HEADS * 2 * LANES).astype(BF16))
            o_a = _gdn(qkv, ba, z, _pad_lanes(gdn_A_log[e], GDN_HEADS), _pad_lanes(gdn_dt_bias[e], GDN_HEADS),
                       gdn_norm_g[e].reshape(1, -1))
            o_b = _flash(q, k, v)
            mix = (mod[l, 1], [o_a, o_b], ev_w_out[e].astype(BF16))
        else:
            o = l // 2
            z, xbc, dt = _odd_in(xs, mod[l, 1], g1, _odd_w_in(ssd_w_in[o]), ssd_conv_w[o],
                                 ssd_conv_b[o].reshape(1, -1))
            y = _ssd(xbc, dt, z, _pad_lanes(ssd_A_log[o], 0), _pad_lanes(ssd_dt_bias[o], 0),
                     jnp.repeat(ssd_D[o].astype(F32), SSD_HEADDIM).reshape(1, -1), ssd_norm_g[o].reshape(1, -1))
            mix = (mod[l, 1], [y], ssd_w_out[o].astype(BF16))
        xs = _ffn(xs, mod[l, 2], norm_g[l, 2].reshape(1, d), w1b, w3b, w2b, l, 1, fg, last, mix=mix)
    return xs.reshape(b, s, d)
```

```python
import functools
import math

import jax
import jax.numpy as jnp
from jax import lax
from jax.experimental import pallas as pl
from jax.experimental.pallas import tpu as pltpu

F32 = jnp.float32
BF16 = jnp.bfloat16

D_MODEL = 1024
DEPTH = 4
CHUNK = 64
NORM_EPS = 1e-6
CONV_K = 4
D_FF = 2816
N_MOD = 9

GDN_HEADS = 4
GDN_DK = 128
GDN_DV = 128
GDN_WIDTH = GDN_HEADS * GDN_DV
GDN_CONV_CH = 2 * GDN_HEADS * GDN_DK + GDN_WIDTH

MLA_HEADS = 4
MLA_Q_RANK = 384
MLA_KV_RANK = 256
MLA_NOPE = 128
MLA_ROPE = 64
MLA_DV = 128
MLA_WIDTH = MLA_HEADS * MLA_DV
MLA_QK = MLA_NOPE + MLA_ROPE
ROPE_THETA = 10000.0

SSD_D_INNER = 2 * D_MODEL
SSD_HEADDIM = 64
SSD_HEADS = SSD_D_INNER // SSD_HEADDIM
SSD_GROUPS = 4
SSD_STATE = 128
SSD_CONV_CH = SSD_D_INNER + 2 * SSD_GROUPS * SSD_STATE
SSD_GROUP_W = SSD_D_INNER // SSD_GROUPS
SSD_HEADS_PER_GROUP = SSD_HEADS // SSD_GROUPS

LANES = 128
SUBLANES = 8
VMEM_LIMIT = 56 * 1024 * 1024
FLASH_T = 512

EV_QKV = 0
EV_Z = GDN_CONV_CH
EV_CQ = EV_Z + GDN_WIDTH
EV_CKV = EV_CQ + MLA_Q_RANK
EV_KR = EV_CKV + MLA_KV_RANK
EV_BA = EV_KR + 2 * MLA_ROPE
EV_IN_P = EV_BA + LANES

OD_Z = 0
OD_XBC = SSD_D_INNER
OD_DT = OD_XBC + SSD_CONV_CH
OD_IN_P = OD_DT + LANES


def _cparams(sem):
    return pltpu.CompilerParams(dimension_semantics=sem, vmem_limit_bytes=VMEM_LIMIT)


def _bdot(a, b):
    return jnp.dot(a.astype(BF16), b.astype(BF16), preferred_element_type=F32)


def _bdot_nt(a, b):
    return lax.dot_general(a.astype(BF16), b.astype(BF16), (((1,), (1,)), ((), ())),
                           preferred_element_type=F32)


def _mask_dot(mask, x):
    m = jnp.where(mask, 1.0, 0.0).astype(BF16)
    out = None
    r = x
    for _ in range(3):
        part = r.astype(BF16)
        r = r - part.astype(F32)
        t = jnp.dot(m, part, preferred_element_type=F32)
        out = t if out is None else out + t
    return out


def _silu(x):
    hx = 0.5 * x
    return hx + hx * jnp.tanh(hx)


def _softplus(x):
    return jnp.maximum(x, 0.0) + jnp.log1p(jnp.exp(-jnp.abs(x)))


def _adaln(x, g, shift, scale):
    ms = jnp.mean(x * x, axis=-1, keepdims=True)
    y = x * lax.rsqrt(ms + NORM_EPS) * g
    return y * (1.0 + scale) + shift


def _mod_kernel(c_ref, w_ref, b_ref, o_ref):
    ca = _silu(c_ref[...])
    cb = jnp.broadcast_to(ca, (D_MODEL, LANES))
    tn = w_ref.shape[2]
    for t in range(tn // LANES):
        sl = slice(t * LANES, (t + 1) * LANES)
        o_ref[0, :, sl] = jnp.sum(w_ref[0, :, sl] * cb, axis=0, keepdims=True) + b_ref[0, :, sl]


def _modulation(c_col, ada_w, ada_b):
    n = N_MOD * D_MODEL
    tn = n // 4
    return pl.pallas_call(
        _mod_kernel,
        grid=(DEPTH, n // tn),
        in_specs=[pl.BlockSpec((D_MODEL, 1), lambda l, j: (0, 0)),
                  pl.BlockSpec((1, D_MODEL, tn), lambda l, j: (l, 0, j)),
                  pl.BlockSpec((1, 1, tn), lambda l, j: (l, 0, j))],
        out_specs=pl.BlockSpec((1, 1, tn), lambda l, j: (l, 0, j)),
        out_shape=jax.ShapeDtypeStruct((DEPTH, 1, n), F32),
        compiler_params=_cparams(("parallel", "parallel")),
        name="modulation",
    )(c_col, ada_w, ada_b.reshape(DEPTH, 1, n))


def _rope_kernel(pos_ref, freq_ref, o_ref):
    ang = pos_ref[...].astype(F32) * freq_ref[...]
    lane = lax.broadcasted_iota(jnp.int32, ang.shape, 1)
    sgn = jnp.where(lane < 3 * (MLA_ROPE // 2), -1.0, 1.0)
    o_ref[...] = jnp.where(lane < MLA_ROPE, jnp.cos(ang), sgn * jnp.sin(ang))


def _rope_table(pos_col):
    s = pos_col.shape[0]
    tm = min(s, 2048)
    half = MLA_ROPE // 2
    inv_freq = ROPE_THETA ** (-jnp.arange(half, dtype=F32) / half)
    freq = jnp.tile(inv_freq, LANES // half).reshape(1, LANES)
    return pl.pallas_call(
        _rope_kernel,
        grid=(s // tm,),
        in_specs=[pl.BlockSpec((tm, 1), lambda i: (i, 0)),
                  pl.BlockSpec((1, LANES), lambda i: (0, 0))],
        out_specs=pl.BlockSpec((tm, LANES), lambda i: (i, 0)),
        out_shape=jax.ShapeDtypeStruct((s, LANES), F32),
        compiler_params=_cparams(("parallel",)),
        name="rope_table",
    )(pos_col, freq)


def _ffn_kernel(*refs, tf, final, n_mix, n_cast):
    x_ref, mod_ref, g_ref, w1_ref, w3_ref, w2_ref, fg_ref = refs[0:7]
    n_extra = n_mix + 2 if n_mix else 0
    mix_refs = refs[7:7 + n_extra]
    rest = refs[7 + n_extra:]
    if n_cast:
        (nw1_ref, nw3_ref, nw2_ref, o_ref, ow1_ref, ow3_ref, ow2_ref, a_ref) = rest
        ow1_ref[...] = nw1_ref[...].astype(BF16)
        ow3_ref[...] = nw3_ref[...].astype(BF16)

        @pl.when(pl.program_id(0) < n_cast)
        def _():
            ow2_ref[...] = nw2_ref[...].astype(BF16)
    else:
        o_ref, a_ref = rest
    x = x_ref[...]
    if n_mix:
        mmod_ref, w_out_ref = mix_refs[0], mix_refs[-1]
        y = None
        k0 = 0
        for act_ref in mix_refs[1:-1]:
            kw = act_ref.shape[1]
            t = jnp.dot(act_ref[...], w_out_ref[k0:k0 + kw, :], preferred_element_type=F32)
            y = t if y is None else y + t
            k0 += kw
        x = x + mmod_ref[2:3, :] * y
    h = _adaln(x, g_ref[...], mod_ref[0:1, :], mod_ref[1:2, :]).astype(BF16)
    for j in range(D_FF // tf):
        sl = slice(j * tf, (j + 1) * tf)
        h1 = jnp.dot(h, w1_ref[:, sl], preferred_element_type=F32)
        h3 = jnp.dot(h, w3_ref[:, sl], preferred_element_type=F32)
        a_ref[:, sl] = (_silu(h1) * h3).astype(BF16)
    y = jnp.dot(a_ref[...], w2_ref[...], preferred_element_type=F32)
    xn = x + (0.5 * mod_ref[2:3, :]) * y
    if final:
        ms = jnp.mean(xn * xn, axis=-1, keepdims=True)
        xn = xn * lax.rsqrt(ms + NORM_EPS) * fg_ref[...]
    o_ref[...] = xn


FFN_W2_CAST_STEPS = 16


def _ffn(x, mod3, g, wb, final_g, final, mix=None, cast_next=None):
    s = x.shape[0]
    tm = min(s, 512)
    n = s // tm
    const = lambda i: (0, 0)
    row = lambda i: (i, 0)
    in_specs = [pl.BlockSpec((tm, D_MODEL), row),
                pl.BlockSpec((3, D_MODEL), const),
                pl.BlockSpec((1, D_MODEL), const),
                pl.BlockSpec((D_MODEL, D_FF), const, pipeline_mode=pl.Buffered(1)),
                pl.BlockSpec((D_MODEL, D_FF), const, pipeline_mode=pl.Buffered(1)),
                pl.BlockSpec((D_FF, D_MODEL), const, pipeline_mode=pl.Buffered(1)),
                pl.BlockSpec((1, D_MODEL), const)]
    args = [x, mod3, g, *wb, final_g]
    n_mix = 0
    if mix is not None:
        mmod, acts, w_out = mix
        n_mix = len(acts)
        in_specs += ([pl.BlockSpec((3, D_MODEL), const)]
                     + [pl.BlockSpec((tm, a.shape[1]), row) for a in acts]
                     + [pl.BlockSpec(w_out.shape, const, pipeline_mode=pl.Buffered(1))])
        args += [mmod, *acts, w_out]
    out_specs = [pl.BlockSpec((tm, D_MODEL), row)]
    out_shape = [jax.ShapeDtypeStruct((s, D_MODEL), F32)]
    n_cast = 0
    if cast_next is not None:
        nw1, nw3, nw2, layer, which = cast_next
        n_cast = min(n, FFN_W2_CAST_STEPS)
        r1, r2 = D_MODEL // n, D_FF // n_cast
        slab = lambda i: (layer, which, i, 0)
        slab2 = lambda i: (layer, which, jnp.minimum(i, n_cast - 1), 0)
        in_specs += [pl.BlockSpec((None, None, r1, D_FF), slab), pl.BlockSpec((None, None, r1, D_FF), slab),
                     pl.BlockSpec((None, None, r2, D_MODEL), slab2)]
        args += [nw1, nw3, nw2]
        out_specs += [pl.BlockSpec((r1, D_FF), row), pl.BlockSpec((r1, D_FF), row),
                      pl.BlockSpec((r2, D_MODEL), lambda i: (jnp.minimum(i, n_cast - 1), 0))]
        out_shape += [jax.ShapeDtypeStruct((D_MODEL, D_FF), BF16), jax.ShapeDtypeStruct((D_MODEL, D_FF), BF16),
                      jax.ShapeDtypeStruct((D_FF, D_MODEL), BF16)]
    out = pl.pallas_call(
        functools.partial(_ffn_kernel, tf=256, final=final, n_mix=n_mix, n_cast=n_cast),
        grid=(n,),
        in_specs=in_specs,
        out_specs=out_specs,
        out_shape=out_shape,
        scratch_shapes=[pltpu.VMEM((tm, D_FF), BF16)],
        compiler_params=_cparams(("arbitrary",)),
        name="ffn_final" if final else ("mix_ffn" if n_mix else "ffn"),
    )(*args)
    return (out[0], tuple(out[1:])) if cast_next is not None else out[0]


def _causal_conv(p, cw, ext_ref, halo_ref, col0):
    tm, w = p.shape
    cs = slice(col0, col0 + w)
    ext_ref[0:SUBLANES, 0:w] = halo_ref[:, cs]
    ext_ref[SUBLANES:SUBLANES + tm, 0:w] = p
    halo_ref[:, cs] = p[tm - SUBLANES:, :]
    y = p * cw[CONV_K - 1:CONV_K, :]
    for j in range(CONV_K - 1):
        off = SUBLANES - (CONV_K - 1) + j
        y = y + ext_ref[off:off + tm, 0:w] * cw[j:j + 1, :]
    return y


def _even_in_kernel(x_ref, mod_ref, g_ref, w_ref, cw_ref, rope_ref, qg_ref, wuq_ref, kvg_ref, wukv_ref,
                    qkv_ref, z_ref, ba_ref, q_ref, k_ref, v_ref, ext_ref, halo_ref):
    @pl.when(pl.program_id(0) == 0)
    def _():
        halo_ref[...] = jnp.zeros_like(halo_ref)

    x = x_ref[...]
    h = _adaln(x, g_ref[...], mod_ref[0:1, :], mod_ref[1:2, :]).astype(BF16)
    proj = lambda c0, w: jnp.dot(h, w_ref[:, c0:c0 + w], preferred_element_type=F32)

    rope = rope_ref[...]
    scale = MLA_QK ** -0.5 * math.log2(math.e)

    def rot(xx):
        yy = xx * rope
        return yy + pltpu.roll(yy, MLA_ROPE, 1)

    def rms(v, g):
        return (v * lax.rsqrt(jnp.mean(v * v, axis=-1, keepdims=True) + NORM_EPS) * g).astype(BF16)

    def gdn_part(part, p):
        c0 = part * GDN_WIDTH
        y = _silu(_causal_conv(p, cw_ref[:, c0:c0 + GDN_WIDTH], ext_ref, halo_ref, c0))
        if part == 2:
            qkv_ref[:, c0:c0 + GDN_WIDTH] = y
            return
        post = GDN_DK ** -0.5 if part == 0 else 1.0
        for hd in range(GDN_HEADS):
            yh = y[:, hd * GDN_DK:(hd + 1) * GDN_DK]
            nrm = lax.rsqrt(jnp.sum(yh * yh, axis=-1, keepdims=True) + NORM_EPS)
            qkv_ref[:, c0 + hd * GDN_DK:c0 + (hd + 1) * GDN_DK] = yh * nrm * post

    def mla_q(qf):
        for hd in range(MLA_HEADS):
            b0 = hd * 2 * LANES
            q_ref[hd, :, 0:MLA_NOPE] = (qf[:, b0:b0 + MLA_NOPE] * scale).astype(BF16)
            qpe = rot(qf[:, b0 + LANES:b0 + 2 * LANES]) * scale
            q_ref[hd, :, MLA_NOPE:MLA_QK] = qpe[:, 0:MLA_ROPE].astype(BF16)

    def mla_kv(kvf, kpe):
        for hd in range(MLA_HEADS):
            b0 = hd * 2 * LANES
            k_ref[hd, :, 0:MLA_NOPE] = kvf[:, b0:b0 + MLA_NOPE].astype(BF16)
            k_ref[hd, :, MLA_NOPE:MLA_QK] = kpe
            v_ref[hd] = kvf[:, b0 + LANES:b0 + 2 * LANES].astype(BF16)

    cq, ckv, kr = proj(EV_CQ, MLA_Q_RANK), proj(EV_CKV, MLA_KV_RANK), proj(EV_KR, 2 * MLA_ROPE)
    p0 = proj(0, GDN_WIDTH)
    cqn, ckvn = rms(cq, qg_ref[...]), rms(ckv, kvg_ref[...])
    kpe = rot(kr)[:, 0:MLA_ROPE].astype(BF16)
    qf = jnp.dot(cqn, wuq_ref[...], preferred_element_type=F32)
    gdn_part(0, p0)
    p1 = proj(GDN_WIDTH, GDN_WIDTH)
    mla_q(qf)
    kvf = jnp.dot(ckvn, wukv_ref[...], preferred_element_type=F32)
    gdn_part(1, p1)
    p2 = proj(2 * GDN_WIDTH, GDN_WIDTH)
    mla_kv(kvf, kpe)
    zb = proj(EV_Z, GDN_WIDTH), proj(EV_BA, LANES)
    gdn_part(2, p2)
    z_ref[...], ba_ref[...] = zb


def _even_in(x, mod3, g, w_in_p, conv_w, rope, qg, wuq_p, kvg, wukv_p):
    s = x.shape[0]
    tm = min(s, 512)
    const = lambda i: (0, 0)
    row = lambda i: (i, 0)
    hrow = lambda i: (0, i, 0)
    return pl.pallas_call(
        _even_in_kernel,
        grid=(s // tm,),
        in_specs=[pl.BlockSpec((tm, D_MODEL), row),
                  pl.BlockSpec((3, D_MODEL), const),
                  pl.BlockSpec((1, D_MODEL), const),
                  pl.BlockSpec((D_MODEL, EV_IN_P), const),
                  pl.BlockSpec((CONV_K, GDN_CONV_CH), const),
                  pl.BlockSpec((tm, LANES), row),
                  pl.BlockSpec((1, MLA_Q_RANK), const),
                  pl.BlockSpec((MLA_Q_RANK, MLA_HEADS * 2 * LANES), const),
                  pl.BlockSpec((1, MLA_KV_RANK), const),
                  pl.BlockSpec((MLA_KV_RANK, MLA_HEADS * 2 * LANES), const)],
        out_specs=[pl.BlockSpec((tm, GDN_CONV_CH), row),
                   pl.BlockSpec((tm, GDN_WIDTH), row),
                   pl.BlockSpec((tm, LANES), row),
                   pl.BlockSpec((MLA_HEADS, tm, MLA_QK), hrow),
                   pl.BlockSpec((MLA_HEADS, tm, MLA_QK), hrow),
                   pl.BlockSpec((MLA_HEADS, tm, MLA_DV), hrow)],
        out_shape=[jax.ShapeDtypeStruct((s, GDN_CONV_CH), F32),
                   jax.ShapeDtypeStruct((s, GDN_WIDTH), F32),
                   jax.ShapeDtypeStruct((s, LANES), F32),
                   jax.ShapeDtypeStruct((MLA_HEADS, s, MLA_QK), BF16),
                   jax.ShapeDtypeStruct((MLA_HEADS, s, MLA_QK), BF16),
                   jax.ShapeDtypeStruct((MLA_HEADS, s, MLA_DV), BF16)],
        scratch_shapes=[pltpu.VMEM((tm + SUBLANES, GDN_WIDTH), F32),
                        pltpu.VMEM((SUBLANES, GDN_CONV_CH), F32)],
        compiler_params=_cparams(("arbitrary",)),
        name="even_in",
    )(x, mod3, g, w_in_p, conv_w, rope, qg, wuq_p, kvg, wukv_p)


GDN_SB = 256
GDN_TB = 512


def _gdn_kernel(qkv_ref, ba_ref, z_ref, alog_ref, dtb_ref, gn_ref, o_ref, s_ref):
    tb = GDN_SB
    nch = tb // CHUNK
    subs = range(GDN_TB // GDN_SB)

    @pl.when(pl.program_id(0) == 0)
    def _():
        s_ref[...] = jnp.zeros_like(s_ref)

    lane = lax.broadcasted_iota(jnp.int32, (tb, LANES), 1)
    ri = lax.broadcasted_iota(jnp.int32, (tb, tb), 0)
    ci = lax.broadcasted_iota(jnp.int32, (tb, tb), 1)
    same = (ri // CHUNK) == (ci // CHUNK)
    beta_all, gc_all, gc_t = [], [], []
    for b in subs:
        ba = ba_ref[b * tb:(b + 1) * tb, :]
        beta_all.append(jax.nn.sigmoid(ba))
        g_all = jnp.where((lane >= GDN_HEADS) & (lane < 2 * GDN_HEADS),
                          -jnp.exp(alog_ref[...]) * _softplus(ba + dtb_ref[...]), 0.0)
        gc_all.append(_mask_dot(same & (ci <= ri), g_all))
        gc_t.append(gc_all[b].T)
    gn = gn_ref[...]
    bd_mask = jnp.where(same, 1.0, 0.0).astype(BF16)

    wi = lax.broadcasted_iota(jnp.int32, (CHUNK, tb), 0)
    wl = lax.broadcasted_iota(jnp.int32, (CHUNK, tb), 1)
    wj, wc = wl % CHUNK, wl // CHUNK
    w_lower, w_strict = wj <= wi, wj < wi
    w_eye = jnp.where(wj == wi, 1.0, 0.0)

    def fold(full):
        out = full[0:CHUNK]
        for c in range(1, nch):
            out = jnp.where(wc == c, full[c * CHUNK:(c + 1) * CHUNK], out)
        return out

    def block_diag(xb):
        return jnp.concatenate([xb] * nch, axis=0) * bd_mask

    def split(x):
        hi = x.astype(BF16)
        return hi, (x - hi.astype(F32)).astype(BF16)

    def wide_dot3(xs, b):
        bh, bl = split(b)
        parts = [split(x) for x in xs]
        r = jnp.dot(jnp.concatenate([t for hl in parts for t in hl], axis=0), block_diag(bh),
                    preferred_element_type=F32)
        r2 = jnp.dot(jnp.concatenate([hl[0] for hl in parts], axis=0), block_diag(bl),
                     preferred_element_type=F32)
        return [r[2 * i * CHUNK:(2 * i + 1) * CHUNK] + r[(2 * i + 1) * CHUNK:(2 * i + 2) * CHUNK]
                + r2[i * CHUNK:(i + 1) * CHUNK] for i in range(len(xs))]

    heads = range(GDN_HEADS)
    units = [(b, hd) for b in subs for hd in heads]
    hv = []
    for b, hd in units:
        rows = slice(b * tb, (b + 1) * tb)
        gcol = gc_all[b][:, GDN_HEADS + hd:GDN_HEADS + hd + 1]
        grow = gc_t[b][GDN_HEADS + hd:GDN_HEADS + hd + 1, :]
        gcol_w = fold(jnp.broadcast_to(gcol, (tb, tb)))
        decay = jnp.where(w_lower, jnp.exp(jnp.where(w_lower, gcol_w - grow, 0.0)), 0.0)
        beta = beta_all[b][:, hd:hd + 1]
        q = qkv_ref[rows, hd * GDN_DK:(hd + 1) * GDN_DK]
        k = qkv_ref[rows, GDN_WIDTH + hd * GDN_DK:GDN_WIDTH + (hd + 1) * GDN_DK]
        v = qkv_ref[rows, 2 * GDN_WIDTH + hd * GDN_DV:2 * GDN_WIDTH + (hd + 1) * GDN_DV]
        kb = k * beta
        egc = jnp.exp(gcol)
        gl = jnp.concatenate([jnp.broadcast_to(gcol[(c + 1) * CHUNK - 1:(c + 1) * CHUNK, :], (CHUNK, 1))
                              for c in range(nch)], axis=0)
        hv.append(dict(decay=decay, q=q, k=k, kb=kb, qs=q * egc, egl=jnp.exp(gl),
                       rhs=jnp.concatenate([v * beta, kb * egc], axis=1).astype(BF16),
                       kend_t=(k * jnp.exp(gl - gcol)).T))

    pw = [jnp.where(w_strict, -(fold(_bdot_nt(h["kb"], h["k"])) * h["decay"]), 0.0) for h in hv]
    tw = [w_eye + p for p in pw]
    pw = [wide_dot3([p], p)[0] for p in pw]
    for _ in range(4):
        res = [wide_dot3([t, p], p) for t, p in zip(tw, pw)]
        tw = [t + r[0] for t, r in zip(tw, res)]
        pw = [r[1] for r in res]
    tw = [t + wide_dot3([t], p)[0] for t, p in zip(tw, pw)]
    pre = {}
    for i, unit in enumerate(units):
        h = hv[i]
        sol = jnp.dot(block_diag(tw[i].astype(BF16)), h["rhs"], preferred_element_type=F32)
        attn = fold(_bdot_nt(h["q"], h["k"])) * h["decay"]
        pre[unit] = (sol[:, 0:GDN_DV], sol[:, GDN_DV:], attn, h["qs"], h["kend_t"], h["egl"])

    col_chunk = lax.broadcasted_iota(jnp.int32, (GDN_DK, tb), 1) // CHUNK
    states = [s_ref[hd] for hd in heads]
    for b in subs:
        for c in range(nch):
            rs = slice(c * CHUNK, (c + 1) * CHUNK)
            pc = [pre[(b, hd)] for hd in heads]
            wss = [_bdot(jnp.concatenate([pc[hd][1][rs], pc[hd][3][rs]], axis=0), states[hd]) for hd in heads]
            v_full = []
            for hd in heads:
                parts = [jnp.zeros((CHUNK, GDN_DV), F32)] * nch
                parts[c] = pc[hd][0][rs] - wss[hd][0:CHUNK]
                v_full.append(jnp.concatenate(parts, axis=0))
            states = [states[hd] * pc[hd][5][c * CHUNK:c * CHUNK + 1, :]
                      + _bdot(jnp.where(col_chunk == c, pc[hd][4], 0.0), v_full[hd]) for hd in heads]
            out_rows = slice(b * tb + c * CHUNK, b * tb + (c + 1) * CHUNK)
            for hd in heads:
                o = wss[hd][CHUNK:] + _bdot(jnp.where(wc == c, pc[hd][2], 0.0), v_full[hd])
                ms = jnp.mean(o * o, axis=-1, keepdims=True)
                zz = z_ref[out_rows, hd * GDN_DV:(hd + 1) * GDN_DV]
                o_ref[out_rows, hd * GDN_DV:(hd + 1) * GDN_DV] = (
                    o * lax.rsqrt(ms + NORM_EPS) * gn * _silu(zz)).astype(BF16)
    for hd in heads:
        s_ref[hd] = states[hd]


def _gdn(qkv, ba, z, alog_row, dtb_row, gn):
    s = qkv.shape[0]
    tb = GDN_TB
    const = lambda i: (0, 0)
    row = lambda i: (i, 0)
    return pl.pallas_call(
        _gdn_kernel,
        grid=(s // tb,),
        in_specs=[pl.BlockSpec((tb, GDN_CONV_CH), row),
                  pl.BlockSpec((tb, LANES), row),
                  pl.BlockSpec((tb, GDN_WIDTH), row),
                  pl.BlockSpec((1, LANES), const),
                  pl.BlockSpec((1, LANES), const),
                  pl.BlockSpec((1, GDN_DV), const)],
        out_specs=pl.BlockSpec((tb, GDN_WIDTH), row),
        out_shape=jax.ShapeDtypeStruct((s, GDN_WIDTH), BF16),
        scratch_shapes=[pltpu.VMEM((GDN_HEADS, GDN_DK, GDN_DV), F32)],
        compiler_params=_cparams(("arbitrary",)),
        name="gdn",
    )(qkv, ba, z, alog_row, dtb_row, gn)


def _flash_kernel(q_ref, k_ref, v_ref, o_ref, m_ref, l_ref, acc_ref):
    t = FLASH_T
    qi = pl.program_id(1)
    nrep = t // LANES
    m_ref[...] = jnp.full_like(m_ref, -jnp.inf)
    l_ref[...] = jnp.zeros_like(l_ref)
    acc_ref[...] = jnp.zeros_like(acc_ref)

    def attend(kb, halves):
        start = pl.multiple_of(kb * t, t)
        k = k_ref[0, pl.ds(start, t), :]
        v = v_ref[0, pl.ds(start, t), :]
        scores = []
        for half, diagonal in halves:
            q = q_ref[0, half * t:(half + 1) * t, :]
            s = lax.dot_general(q, k, (((1,), (1,)), ((), ())), preferred_element_type=F32)
            if diagonal:
                qc = lax.broadcasted_iota(jnp.int32, (t, t), 0) // CHUNK
                kc = lax.broadcasted_iota(jnp.int32, (t, t), 1) // CHUNK
                s = jnp.where(kc <= qc, s, -jnp.inf)
            scores.append(s)
        probs = []
        for (half, _), s in zip(halves, scores):
            m_old = m_ref[half]
            m_new = jnp.maximum(m_old, jnp.max(s, axis=-1, keepdims=True))
            alpha = jnp.exp2(m_old - m_new)
            p = jnp.exp2(s - jnp.concatenate([m_new] * nrep, axis=1))
            psum = p[:, 0:LANES]
            for r in range(1, nrep):
                psum = psum + p[:, r * LANES:(r + 1) * LANES]
            l_ref[half] = l_ref[half] * alpha + psum
            m_ref[half] = m_new
            probs.append((alpha, p.astype(BF16)))
        for (half, _), (alpha, p) in zip(halves, probs):
            acc_ref[half] = acc_ref[half] * alpha + jnp.dot(p, v, preferred_element_type=F32)

    def body(i, carry):
        for kb in range(4):
            attend(4 * i + kb, [(0, False), (1, False)])
        return carry

    lax.fori_loop(0, qi // 2, body, 0)

    @pl.when(qi % 2 == 1)
    def _():
        for kb in (2 * qi - 2, 2 * qi - 1):
            attend(kb, [(0, False), (1, False)])

    attend(2 * qi, [(0, True), (1, False)])
    attend(2 * qi + 1, [(1, True)])
    for half in range(2):
        l = jnp.sum(l_ref[half], axis=-1, keepdims=True)
        o_ref[half * t:(half + 1) * t, :] = (acc_ref[half] / l).astype(o_ref.dtype)


def _flash(q, k, v):
    nh, s, _ = q.shape
    t = FLASH_T
    assert s % (2 * t) == 0
    return pl.pallas_call(
        _flash_kernel,
        grid=(nh, s // (2 * t)),
        in_specs=[pl.BlockSpec((1, 2 * t, MLA_QK), lambda h, qi: (h, qi, 0)),
                  pl.BlockSpec((1, s, MLA_QK), lambda h, qi: (h, 0, 0)),
                  pl.BlockSpec((1, s, MLA_DV), lambda h, qi: (h, 0, 0))],
        out_specs=pl.BlockSpec((2 * t, MLA_DV), lambda h, qi: (qi, h)),
        out_shape=jax.ShapeDtypeStruct((s, MLA_WIDTH), BF16),
        scratch_shapes=[pltpu.VMEM((2, t, LANES), F32), pltpu.VMEM((2, t, LANES), F32),
                        pltpu.VMEM((2, t, MLA_DV), F32)],
        compiler_params=_cparams(("parallel", "arbitrary")),
        name="mla_flash",
    )(q, k, v)


def _odd_in_kernel(x_ref, mod_ref, g_ref, w_ref, cw_ref, cb_ref, z_ref, xbc_ref, dt_ref, ext_ref, halo_ref, *, tc):
    @pl.when(pl.program_id(0) == 0)
    def _():
        halo_ref[...] = jnp.zeros_like(halo_ref)

    x = x_ref[...]
    h = _adaln(x, g_ref[...], mod_ref[0:1, :], mod_ref[1:2, :]).astype(BF16)
    proj = lambda c0, w: jnp.dot(h, w_ref[:, c0:c0 + w], preferred_element_type=F32)

    def finish(kind, c0, p):
        if kind == "z":
            z_ref[:, c0:c0 + tc] = p
        elif kind == "dt":
            dt_ref[...] = p
        else:
            y = _causal_conv(p, cw_ref[:, c0:c0 + tc], ext_ref, halo_ref, c0)
            xbc_ref[:, c0:c0 + tc] = _silu(y + cb_ref[:, c0:c0 + tc])

    nz, nx = SSD_D_INNER // tc, SSD_CONV_CH // tc
    tasks = []
    for j in range(max(nz, nx)):
        if j < nx:
            tasks.append(("xbc", j * tc, OD_XBC + j * tc, tc))
        if j < nz:
            tasks.append(("z", j * tc, OD_Z + j * tc, tc))
    tasks.append(("dt", 0, OD_DT, LANES))
    pending = None
    for kind, c0, wc0, width in tasks:
        p = proj(wc0, width)
        if pending is not None:
            finish(*pending)
        pending = (kind, c0, p)
    finish(*pending)


def _odd_in(x, mod3, g, w_in_p, conv_w, conv_b):
    s = x.shape[0]
    tm = min(s, 512)
    tc = 512
    const = lambda i: (0, 0)
    row = lambda i: (i, 0)
    return pl.pallas_call(
        functools.partial(_odd_in_kernel, tc=tc),
        grid=(s // tm,),
        in_specs=[pl.BlockSpec((tm, D_MODEL), row),
                  pl.BlockSpec((3, D_MODEL), const),
                  pl.BlockSpec((1, D_MODEL), const),
                  pl.BlockSpec((D_MODEL, OD_IN_P), const),
                  pl.BlockSpec((CONV_K, SSD_CONV_CH), const),
                  pl.BlockSpec((1, SSD_CONV_CH), const)],
        out_specs=[pl.BlockSpec((tm, SSD_D_INNER), row),
                   pl.BlockSpec((tm, SSD_CONV_CH), row),
                   pl.BlockSpec((tm, LANES), row)],
        out_shape=[jax.ShapeDtypeStruct((s, SSD_D_INNER), F32),
                   jax.ShapeDtypeStruct((s, SSD_CONV_CH), F32),
                   jax.ShapeDtypeStruct((s, LANES), F32)],
        scratch_shapes=[pltpu.VMEM((tm + SUBLANES, tc), F32),
                        pltpu.VMEM((SUBLANES, SSD_CONV_CH), F32)],
        compiler_params=_cparams(("arbitrary",)),
        name="odd_in",
    )(x, mod3, g, w_in_p, conv_w, conv_b)


SSD_TB = 256
SSD_CHUNK = 128


def _ssd_kernel(xbc_ref, dt_ref, z_ref, alog_ref, dtb_ref, dskip_ref, ng_ref, o_ref, hs_ref):
    tb, lc = SSD_TB, SSD_CHUNK

    @pl.when(pl.program_id(0) == 0)
    def _():
        hs_ref[...] = jnp.zeros_like(hs_ref)

    lane = lax.broadcasted_iota(jnp.int32, (tb, LANES), 1)
    dt = jnp.where(lane < SSD_HEADS, _softplus(dt_ref[...] + dtb_ref[...]), 0.0)
    da = dt * (-jnp.exp(alog_ref[...]))
    ri = lax.broadcasted_iota(jnp.int32, (tb, tb), 0)
    ci = lax.broadcasted_iota(jnp.int32, (tb, tb), 1)
    acs = _mask_dot(((ri // lc) == (ci // lc)) & (ci <= ri), da)
    acs = acs * math.log2(math.e)
    acs_t = acs.T
    ldt_t = jnp.log2(jnp.where(lane < SSD_HEADS, dt, 1.0)).T
    left = lax.broadcasted_iota(jnp.int32, (lc, LANES), 1) < SSD_HEADDIM
    halves = (left, jnp.logical_not(left))
    causal = jnp.where(lax.broadcasted_iota(jnp.int32, (lc, lc), 1) <= lax.broadcasted_iota(jnp.int32, (lc, lc), 0),
                       0.0, -1e30)
    groups = range(SSD_GROUPS)
    pairs = range(SSD_HEADS_PER_GROUP // 2)

    def gcols(g):
        return slice(g * SSD_GROUP_W, (g + 1) * SSD_GROUP_W)

    def pcols(pr):
        return slice(pr * LANES, (pr + 1) * LANES)

    states = [[hs_ref[g, :, pcols(pr)] for pr in pairs] for g in groups]
    for sc in range(tb // lc):
        rs = slice(sc * lc, (sc + 1) * lc)
        acs_c, acs_tc, ldt_tc = acs[rs], acs_t[:, rs], ldt_t[:, rs]
        a_last = acs_c[lc - 1:lc, :]
        c_decay = jnp.exp2(a_last)
        xs = [xbc_ref[rs, gcols(g)] for g in groups]
        bg_t = [xbc_ref[rs, SSD_D_INNER + g * SSD_STATE:SSD_D_INNER + (g + 1) * SSD_STATE].T for g in groups]
        cg = [xbc_ref[rs, SSD_D_INNER + (SSD_GROUPS + g) * SSD_STATE:SSD_D_INNER + (SSD_GROUPS + g + 1) * SSD_STATE]
              for g in groups]
        cb = [lax.dot_general(cg[g].astype(BF16), bg_t[g].astype(BF16), (((1,), (0,)), ((), ())),
                              preferred_element_type=F32) for g in groups]
        ys = [[None] * len(pairs) for _ in groups]
        new_states = [[None] * len(pairs) for _ in groups]
        for pr in pairs:
            for g in groups:
                x_pair = xs[g][:, pcols(pr)]
                st = states[g][pr]
                acc = None
                upd = None
                for half in range(2):
                    hd = g * SSD_HEADS_PER_GROUP + 2 * pr + half
                    colb = jnp.broadcast_to(acs_c[:, hd:hd + 1], (lc, LANES))
                    lrow = ldt_tc[hd:hd + 1, :] - acs_tc[hd:hd + 1, :]
                    m1 = cb[g] * jnp.exp2(colb + (lrow + causal))
                    c1 = cg[g] * jnp.exp2(colb)
                    xh = jnp.where(halves[half], x_pair, 0.0).astype(BF16)
                    sh = jnp.where(halves[half], st, 0.0).astype(BF16)
                    t = jnp.dot(jnp.concatenate([m1, c1], axis=1).astype(BF16), jnp.concatenate([xh, sh], axis=0),
                                preferred_element_type=F32)
                    acc = t if acc is None else acc + t
                    wrow = jnp.exp2(a_last[:, hd:hd + 1] + lrow)
                    u = jnp.dot((bg_t[g] * wrow).astype(BF16), xh, preferred_element_type=F32)
                    upd = u if upd is None else upd + u
                ys[g][pr] = acc
                h0 = g * SSD_HEADS_PER_GROUP + 2 * pr
                cdec = jnp.where(left[0:1], c_decay[:, h0:h0 + 1], c_decay[:, h0 + 1:h0 + 2])
                new_states[g][pr] = st * cdec + upd
        states = new_states
        for g in groups:
            y = jnp.concatenate(ys[g], axis=1) + dskip_ref[:, gcols(g)] * xs[g]
            y = y * _silu(z_ref[rs, gcols(g)])
            ms = jnp.mean(y * y, axis=-1, keepdims=True)
            o_ref[rs, gcols(g)] = (y * lax.rsqrt(ms + NORM_EPS) * ng_ref[:, gcols(g)]).astype(BF16)
    for g in groups:
        for pr in pairs:
            hs_ref[g, :, pcols(pr)] = states[g][pr]


def _ssd(xbc, dt, z, alog_row, dtb_row, dskip_row, ng):
    s = xbc.shape[0]
    tb = SSD_TB
    const = lambda i: (0, 0)
    row = lambda i: (i, 0)
    return pl.pallas_call(
        _ssd_kernel,
        grid=(s // tb,),
        in_specs=[pl.BlockSpec((tb, SSD_CONV_CH), row),
                  pl.BlockSpec((tb, LANES), row),
                  pl.BlockSpec((tb, SSD_D_INNER), row),
                  pl.BlockSpec((1, LANES), const),
                  pl.BlockSpec((1, LANES), const),
                  pl.BlockSpec((1, SSD_D_INNER), const),
                  pl.BlockSpec((1, SSD_D_INNER), const)],
        out_specs=pl.BlockSpec((tb, SSD_D_INNER), row),
        out_shape=jax.ShapeDtypeStruct((s, SSD_D_INNER), BF16),
        scratch_shapes=[pltpu.VMEM((SSD_GROUPS, SSD_STATE, SSD_GROUP_W), F32)],
        compiler_params=_cparams(("arbitrary",)),
        name="ssd",
    )(xbc, dt, z, alog_row, dtb_row, dskip_row, ng)


def _pad_lanes(v, lane0):
    return jnp.zeros((1, LANES), F32).at[0, lane0:lane0 + v.shape[0]].set(v.astype(F32))


def _swap_halves(w):
    half = w.shape[-1] // 2
    return jnp.concatenate([w[..., half:], w[..., :half]], axis=-1)


def _even_w_in(w):
    qkv, z, beta, a, cq, ckv, kr = jnp.split(
        w, [GDN_CONV_CH, GDN_CONV_CH + GDN_WIDTH, GDN_CONV_CH + GDN_WIDTH + GDN_HEADS,
            GDN_CONV_CH + GDN_WIDTH + 2 * GDN_HEADS, GDN_CONV_CH + GDN_WIDTH + 2 * GDN_HEADS + MLA_Q_RANK,
            GDN_CONV_CH + GDN_WIDTH + 2 * GDN_HEADS + MLA_Q_RANK + MLA_KV_RANK], axis=1)
    pad = jnp.zeros((D_MODEL, LANES - 2 * GDN_HEADS), w.dtype)
    return jnp.concatenate([qkv, z, cq, ckv, kr, _swap_halves(kr), beta, a, pad], axis=1).astype(BF16)


def _even_w_uq(w):
    pe = w[..., MLA_NOPE:]
    return jnp.concatenate([w, _swap_halves(pe)], axis=-1).reshape(MLA_Q_RANK, MLA_HEADS * 2 * LANES).astype(BF16)


def _odd_w_in(w):
    pad = jnp.zeros((D_MODEL, LANES - SSD_HEADS), w.dtype)
    return jnp.concatenate([w, pad], axis=1).astype(BF16)


def kernel(x, c, positions, ada_w, ada_b, norm_g, ffn_w1, ffn_w3, ffn_w2, ev_w_in, gdn_conv_w, gdn_A_log, gdn_dt_bias, gdn_norm_g, mla_q_norm_g, mla_w_uq, mla_kv_norm_g, mla_w_ukv, ev_w_out, ssd_w_in, ssd_conv_w, ssd_conv_b, ssd_A_log, ssd_dt_bias, ssd_D, ssd_norm_g, ssd_w_out, final_g):
    b, s, d = x.shape
    assert b == 1 and d == D_MODEL and s % (2 * FLASH_T) == 0
    xs = x.reshape(s, d)
    mod = _modulation(c.reshape(d, 1), ada_w, ada_b).reshape(DEPTH, 3, 3, d)
    rope = _rope_table(positions.reshape(s, 1))
    fg = final_g.reshape(1, d)
    wb = (ffn_w1[0, 0].astype(BF16), ffn_w3[0, 0].astype(BF16), ffn_w2[0, 0].astype(BF16))
    for l in range(DEPTH):
        last = l == DEPTH - 1
        xs, wb = _ffn(xs, mod[l, 0], norm_g[l, 0].reshape(1, d), wb, fg, False,
                      cast_next=(ffn_w1, ffn_w3, ffn_w2, l, 1))
        g1 = norm_g[l, 1].reshape(1, d)
        if l % 2 == 0:
            e = l // 2
            qkv, z, ba, q, k, v = _even_in(
                xs, mod[l, 1], g1, _even_w_in(ev_w_in[e]), gdn_conv_w[e], rope,
                mla_q_norm_g[e].reshape(1, -1), _even_w_uq(mla_w_uq[e]),
                mla_kv_norm_g[e].reshape(1, -1),
                mla_w_ukv[e].reshape(MLA_KV_RANK, MLA_HEADS * 2 * LANES).astype(BF16))
            o_a = _gdn(qkv, ba, z, _pad_lanes(gdn_A_log[e], GDN_HEADS), _pad_lanes(gdn_dt_bias[e], GDN_HEADS),
                       gdn_norm_g[e].reshape(1, -1))
            o_b = _flash(q, k, v)
            mix = (mod[l, 1], [o_a, o_b], ev_w_out[e].astype(BF16))
        else:
            o = l // 2
            z, xbc, dt = _odd_in(xs, mod[l, 1], g1, _odd_w_in(ssd_w_in[o]), ssd_conv_w[o],
                                 ssd_conv_b[o].reshape(1, -1))
            y = _ssd(xbc, dt, z, _pad_lanes(ssd_A_log[o], 0), _pad_lanes(ssd_dt_bias[o], 0),
                     jnp.repeat(ssd_D[o].astype(F32), SSD_HEADDIM).reshape(1, -1), ssd_norm_g[o].reshape(1, -1))
            mix = (mod[l, 1], [y], ssd_w_out[o].astype(BF16))
        if last:
            xs = _ffn(xs, mod[l, 2], norm_g[l, 2].reshape(1, d), wb, fg, True, mix=mix)
        else:
            xs, wb = _ffn(xs, mod[l, 2], norm_g[l, 2].reshape(1, d), wb, fg, False, mix=mix,
                          cast_next=(ffn_w1, ffn_w3, ffn_w2, l + 1, 0))
    return xs.reshape(b, s, d)
```

```python
import functools
import math

import jax
import jax.numpy as jnp
from jax import lax
from jax.experimental import pallas as pl
from jax.experimental.pallas import tpu as pltpu

F32 = jnp.float32
BF16 = jnp.bfloat16

D_MODEL = 1024
DEPTH = 4
CHUNK = 64
NORM_EPS = 1e-6
CONV_K = 4
D_FF = 2816
N_MOD = 9

GDN_HEADS = 4
GDN_DK = 128
GDN_DV = 128
GDN_WIDTH = GDN_HEADS * GDN_DV
GDN_CONV_CH = 2 * GDN_HEADS * GDN_DK + GDN_WIDTH

MLA_HEADS = 4
MLA_Q_RANK = 384
MLA_KV_RANK = 256
MLA_NOPE = 128
MLA_ROPE = 64
MLA_DV = 128
MLA_WIDTH = MLA_HEADS * MLA_DV
MLA_QK = MLA_NOPE + MLA_ROPE
ROPE_THETA = 10000.0

SSD_D_INNER = 2 * D_MODEL
SSD_HEADDIM = 64
SSD_HEADS = SSD_D_INNER // SSD_HEADDIM
SSD_GROUPS = 4
SSD_STATE = 128
SSD_CONV_CH = SSD_D_INNER + 2 * SSD_GROUPS * SSD_STATE
SSD_GROUP_W = SSD_D_INNER // SSD_GROUPS
SSD_HEADS_PER_GROUP = SSD_HEADS // SSD_GROUPS

LANES = 128
SUBLANES = 8
VMEM_LIMIT = 56 * 1024 * 1024
FLASH_T = 512

EV_QKV = 0
EV_Z = GDN_CONV_CH
EV_CQ = EV_Z + GDN_WIDTH
EV_CKV = EV_CQ + MLA_Q_RANK
EV_KR = EV_CKV + MLA_KV_RANK
EV_BA = EV_KR + 2 * MLA_ROPE
EV_IN_P = EV_BA + LANES

OD_Z = 0
OD_XBC = SSD_D_INNER
OD_IN = OD_XBC + SSD_CONV_CH + SSD_HEADS
OD_DT = OD_IN - SSD_HEADS - LANES


def _cparams(sem):
    return pltpu.CompilerParams(dimension_semantics=sem, vmem_limit_bytes=VMEM_LIMIT)


def _bdot(a, b):
    return jnp.dot(a.astype(BF16), b.astype(BF16), preferred_element_type=F32)


def _bdot_nt(a, b):
    return lax.dot_general(a.astype(BF16), b.astype(BF16), (((1,), (1,)), ((), ())),
                           preferred_element_type=F32)


def _mask_dot(mask, x):
    m = jnp.where(mask, 1.0, 0.0).astype(BF16)
    out = None
    r = x
    for _ in range(3):
        part = r.astype(BF16)
        r = r - part.astype(F32)
        t = jnp.dot(m, part, preferred_element_type=F32)
        out = t if out is None else out + t
    return out


def _silu(x):
    hx = 0.5 * x
    return hx + hx * jnp.tanh(hx)


def _softplus(x):
    return jnp.maximum(x, 0.0) + jnp.log1p(jnp.exp(-jnp.abs(x)))


def _adaln(x, g, shift, scale):
    ms = jnp.mean(x * x, axis=-1, keepdims=True)
    y = x * lax.rsqrt(ms + NORM_EPS) * g
    return y * (1.0 + scale) + shift


def _mod_kernel(c_ref, w_ref, b_ref, o_ref):
    ca = _silu(c_ref[...])
    cb = jnp.broadcast_to(ca, (D_MODEL, LANES))
    tn = w_ref.shape[2]
    for t in range(tn // LANES):
        sl = slice(t * LANES, (t + 1) * LANES)
        o_ref[0, :, sl] = jnp.sum(w_ref[0, :, sl] * cb, axis=0, keepdims=True) + b_ref[0, :, sl]


def _modulation(c_col, ada_w, ada_b):
    n = N_MOD * D_MODEL
    tn = n // 4
    return pl.pallas_call(
        _mod_kernel,
        grid=(DEPTH, n // tn),
        in_specs=[pl.BlockSpec((D_MODEL, 1), lambda l, j: (0, 0)),
                  pl.BlockSpec((1, D_MODEL, tn), lambda l, j: (l, 0, j)),
                  pl.BlockSpec((1, 1, tn), lambda l, j: (l, 0, j))],
        out_specs=pl.BlockSpec((1, 1, tn), lambda l, j: (l, 0, j)),
        out_shape=jax.ShapeDtypeStruct((DEPTH, 1, n), F32),
        compiler_params=_cparams(("parallel", "parallel")),
        name="modulation",
    )(c_col, ada_w, ada_b.reshape(DEPTH, 1, n))


def _rope_kernel(pos_ref, freq_ref, o_ref):
    ang = pos_ref[...].astype(F32) * freq_ref[...]
    lane = lax.broadcasted_iota(jnp.int32, ang.shape, 1)
    sgn = jnp.where(lane < 3 * (MLA_ROPE // 2), -1.0, 1.0)
    o_ref[...] = jnp.where(lane < MLA_ROPE, jnp.cos(ang), sgn * jnp.sin(ang))


def _rope_table(pos_col):
    s = pos_col.shape[0]
    tm = min(s, 2048)
    half = MLA_ROPE // 2
    inv_freq = ROPE_THETA ** (-jnp.arange(half, dtype=F32) / half)
    freq = jnp.tile(inv_freq, LANES // half).reshape(1, LANES)
    return pl.pallas_call(
        _rope_kernel,
        grid=(s // tm,),
        in_specs=[pl.BlockSpec((tm, 1), lambda i: (i, 0)),
                  pl.BlockSpec((1, LANES), lambda i: (0, 0))],
        out_specs=pl.BlockSpec((tm, LANES), lambda i: (i, 0)),
        out_shape=jax.ShapeDtypeStruct((s, LANES), F32),
        compiler_params=_cparams(("parallel",)),
        name="rope_table",
    )(pos_col, freq)


def _ffn_kernel(*refs, tf, final, n_mix, n_cast, n_side):
    x_ref, mod_ref, g_ref, w1_ref, w3_ref, w2_ref, fg_ref = refs[0:7]
    n_extra = n_mix + 2 if n_mix else 0
    mix_refs = refs[7:7 + n_extra]
    n_in = (3 if n_cast else 0) + n_side
    cast_in = refs[7 + n_extra:7 + n_extra + n_in]
    o_ref = refs[7 + n_extra + n_in]
    cast_out = refs[8 + n_extra + n_in:8 + n_extra + 2 * n_in]
    a_ref = refs[8 + n_extra + 2 * n_in]
    for k, (src, dst) in enumerate(zip(cast_in, cast_out)):
        if n_cast and k == 2:
            @pl.when(pl.program_id(0) < n_cast)
            def _():
                dst[...] = src[...].astype(BF16)
        elif src.shape[1] == dst.shape[1]:
            dst[...] = src[...].astype(BF16)
        else:
            dst[...] = _even_w_in(src[...])
    x = x_ref[...]
    if n_mix:
        mmod_ref, w_out_ref = mix_refs[0], mix_refs[-1]
        y = None
        k0 = 0
        for act_ref in mix_refs[1:-1]:
            kw = act_ref.shape[1]
            t = jnp.dot(act_ref[...], w_out_ref[k0:k0 + kw, :], preferred_element_type=F32)
            y = t if y is None else y + t
            k0 += kw
        x = x + mmod_ref[2:3, :] * y
    h = _adaln(x, g_ref[...], mod_ref[0:1, :], mod_ref[1:2, :]).astype(BF16)
    for j in range(D_FF // tf):
        sl = slice(j * tf, (j + 1) * tf)
        h1 = jnp.dot(h, w1_ref[:, sl], preferred_element_type=F32)
        h3 = jnp.dot(h, w3_ref[:, sl], preferred_element_type=F32)
        a_ref[:, sl] = (_silu(h1) * h3).astype(BF16)
    y = jnp.dot(a_ref[...], w2_ref[...], preferred_element_type=F32)
    xn = x + (0.5 * mod_ref[2:3, :]) * y
    if final:
        ms = jnp.mean(xn * xn, axis=-1, keepdims=True)
        xn = xn * lax.rsqrt(ms + NORM_EPS) * fg_ref[...]
    o_ref[...] = xn


FFN_W2_CAST_STEPS = 16


def _ffn(x, mod3, g, wb, final_g, final, mix=None, cast_next=None, cast_side=()):
    s = x.shape[0]
    tm = min(s, 512)
    n = s // tm
    const = lambda i: (0, 0)
    row = lambda i: (i, 0)
    in_specs = [pl.BlockSpec((tm, D_MODEL), row),
                pl.BlockSpec((3, D_MODEL), const),
                pl.BlockSpec((1, D_MODEL), const),
                pl.BlockSpec((D_MODEL, D_FF), const, pipeline_mode=pl.Buffered(1)),
                pl.BlockSpec((D_MODEL, D_FF), const, pipeline_mode=pl.Buffered(1)),
                pl.BlockSpec((D_FF, D_MODEL), const, pipeline_mode=pl.Buffered(1)),
                pl.BlockSpec((1, D_MODEL), const)]
    args = [x, mod3, g, *wb, final_g]
    n_mix = 0
    if mix is not None:
        mmod, acts, w_out = mix
        n_mix = len(acts)
        in_specs += ([pl.BlockSpec((3, D_MODEL), const)]
                     + [pl.BlockSpec((tm, a.shape[1]), row) for a in acts]
                     + [pl.BlockSpec(w_out.shape, const, pipeline_mode=pl.Buffered(1))])
        args += [mmod, *acts, w_out]
    out_specs = [pl.BlockSpec((tm, D_MODEL), row)]
    out_shape = [jax.ShapeDtypeStruct((s, D_MODEL), F32)]
    n_cast = 0
    if cast_next is not None:
        nw1, nw3, nw2, layer, which = cast_next
        n_cast = min(n, FFN_W2_CAST_STEPS)
        r1, r2 = D_MODEL // n, D_FF // n_cast
        slab = lambda i: (layer, which, i, 0)
        slab2 = lambda i: (layer, which, jnp.minimum(i, n_cast - 1), 0)
        in_specs += [pl.BlockSpec((None, None, r1, D_FF), slab), pl.BlockSpec((None, None, r1, D_FF), slab),
                     pl.BlockSpec((None, None, r2, D_MODEL), slab2)]
        args += [nw1, nw3, nw2]
        out_specs += [pl.BlockSpec((r1, D_FF), row), pl.BlockSpec((r1, D_FF), row),
                      pl.BlockSpec((r2, D_MODEL), lambda i: (jnp.minimum(i, n_cast - 1), 0))]
        out_shape += [jax.ShapeDtypeStruct((D_MODEL, D_FF), BF16), jax.ShapeDtypeStruct((D_MODEL, D_FF), BF16),
                      jax.ShapeDtypeStruct((D_FF, D_MODEL), BF16)]
    for w, idx, out_cols in cast_side:
        rows, cols = w.shape[1:]
        in_specs.append(pl.BlockSpec((None, rows // n, cols), lambda i, idx=idx: (idx, i, 0)))
        args.append(w)
        out_specs.append(pl.BlockSpec((rows // n, out_cols), row))
        out_shape.append(jax.ShapeDtypeStruct((rows, out_cols), BF16))
    out = pl.pallas_call(
        functools.partial(_ffn_kernel, tf=256, final=final, n_mix=n_mix, n_cast=n_cast, n_side=len(cast_side)),
        grid=(n,),
        in_specs=in_specs,
        out_specs=out_specs,
        out_shape=out_shape,
        scratch_shapes=[pltpu.VMEM((tm, D_FF), BF16)],
        compiler_params=_cparams(("arbitrary",)),
        name="ffn_final" if final else ("mix_ffn" if n_mix else "ffn"),
    )(*args)
    if cast_next is None:
        return out[0]
    return out[0], tuple(out[1:4]), list(out[4:])


def _causal_conv(p, cw, ext_ref, halo_ref, col0):
    tm, w = p.shape
    cs = slice(col0, col0 + w)
    ext_ref[0:SUBLANES, 0:w] = halo_ref[:, cs]
    ext_ref[SUBLANES:SUBLANES + tm, 0:w] = p
    halo_ref[:, cs] = p[tm - SUBLANES:, :]
    y = p * cw[CONV_K - 1:CONV_K, :]
    for j in range(CONV_K - 1):
        off = SUBLANES - (CONV_K - 1) + j
        y = y + ext_ref[off:off + tm, 0:w] * cw[j:j + 1, :]
    return y


def _even_in_kernel(x_ref, mod_ref, g_ref, w_ref, cw_ref, rope_ref, qg_ref, wuq_ref, kvg_ref, wukv_ref,
                    qkv_ref, z_ref, ba_ref, q_ref, k_ref, v_ref, ext_ref, halo_ref):
    @pl.when(pl.program_id(0) == 0)
    def _():
        halo_ref[...] = jnp.zeros_like(halo_ref)

    x = x_ref[...]
    h = _adaln(x, g_ref[...], mod_ref[0:1, :], mod_ref[1:2, :]).astype(BF16)
    proj = lambda c0, w: jnp.dot(h, w_ref[:, c0:c0 + w], preferred_element_type=F32)

    rope = rope_ref[...]
    scale = MLA_QK ** -0.5 * math.log2(math.e)

    def rot(xx):
        yy = xx * rope
        return yy + pltpu.roll(yy, MLA_ROPE, 1)

    def rms(v, g):
        return (v * lax.rsqrt(jnp.mean(v * v, axis=-1, keepdims=True) + NORM_EPS) * g).astype(BF16)

    def gdn_part(part, p):
        c0 = part * GDN_WIDTH
        y = _silu(_causal_conv(p, cw_ref[:, c0:c0 + GDN_WIDTH], ext_ref, halo_ref, c0))
        if part == 2:
            qkv_ref[:, c0:c0 + GDN_WIDTH] = y
            return
        post = GDN_DK ** -0.5 if part == 0 else 1.0
        for hd in range(GDN_HEADS):
            yh = y[:, hd * GDN_DK:(hd + 1) * GDN_DK]
            nrm = lax.rsqrt(jnp.sum(yh * yh, axis=-1, keepdims=True) + NORM_EPS)
            qkv_ref[:, c0 + hd * GDN_DK:c0 + (hd + 1) * GDN_DK] = yh * nrm * post

    def mla_q(qf):
        for hd in range(MLA_HEADS):
            b0 = hd * 2 * LANES
            q_ref[hd, :, 0:MLA_NOPE] = (qf[:, b0:b0 + MLA_NOPE] * scale).astype(BF16)
            qpe = rot(qf[:, b0 + LANES:b0 + 2 * LANES]) * scale
            q_ref[hd, :, MLA_NOPE:MLA_QK] = qpe[:, 0:MLA_ROPE].astype(BF16)

    def mla_kv(kvf, kpe):
        for hd in range(MLA_HEADS):
            b0 = hd * 2 * LANES
            k_ref[hd, :, 0:MLA_NOPE] = kvf[:, b0:b0 + MLA_NOPE].astype(BF16)
            k_ref[hd, :, MLA_NOPE:MLA_QK] = kpe
            v_ref[hd] = kvf[:, b0 + LANES:b0 + 2 * LANES].astype(BF16)

    cq, ckv, kr = proj(EV_CQ, MLA_Q_RANK), proj(EV_CKV, MLA_KV_RANK), proj(EV_KR, 2 * MLA_ROPE)
    p0 = proj(0, GDN_WIDTH)
    cqn, ckvn = rms(cq, qg_ref[...]), rms(ckv, kvg_ref[...])
    kpe = rot(kr)[:, 0:MLA_ROPE].astype(BF16)
    qf = jnp.dot(cqn, wuq_ref[...], preferred_element_type=F32)
    gdn_part(0, p0)
    p1 = proj(GDN_WIDTH, GDN_WIDTH)
    mla_q(qf)
    kvf = jnp.dot(ckvn, wukv_ref[...], preferred_element_type=F32)
    gdn_part(1, p1)
    p2 = proj(2 * GDN_WIDTH, GDN_WIDTH)
    mla_kv(kvf, kpe)
    zb = proj(EV_Z, GDN_WIDTH), proj(EV_BA, LANES)
    gdn_part(2, p2)
    z_ref[...], ba_ref[...] = zb


def _even_in(x, mod3, g, w_in_p, conv_w, rope, qg, wuq_p, kvg, wukv_p):
    s = x.shape[0]
    tm = min(s, 512)
    const = lambda i: (0, 0)
    row = lambda i: (i, 0)
    hrow = lambda i: (0, i, 0)
    return pl.pallas_call(
        _even_in_kernel,
        grid=(s // tm,),
        in_specs=[pl.BlockSpec((tm, D_MODEL), row),
                  pl.BlockSpec((3, D_MODEL), const),
                  pl.BlockSpec((1, D_MODEL), const),
                  pl.BlockSpec((D_MODEL, EV_IN_P), const),
                  pl.BlockSpec((CONV_K, GDN_CONV_CH), const),
                  pl.BlockSpec((tm, LANES), row),
                  pl.BlockSpec((1, MLA_Q_RANK), const),
                  pl.BlockSpec((MLA_Q_RANK, MLA_HEADS * 2 * LANES), const),
                  pl.BlockSpec((1, MLA_KV_RANK), const),
                  pl.BlockSpec((MLA_KV_RANK, MLA_HEADS * 2 * LANES), const)],
        out_specs=[pl.BlockSpec((tm, GDN_CONV_CH), row),
                   pl.BlockSpec((tm, GDN_WIDTH), row),
                   pl.BlockSpec((tm, LANES), row),
                   pl.BlockSpec((MLA_HEADS, tm, MLA_QK), hrow),
                   pl.BlockSpec((MLA_HEADS, tm, MLA_QK), hrow),
                   pl.BlockSpec((MLA_HEADS, tm, MLA_DV), hrow)],
        out_shape=[jax.ShapeDtypeStruct((s, GDN_CONV_CH), F32),
                   jax.ShapeDtypeStruct((s, GDN_WIDTH), F32),
                   jax.ShapeDtypeStruct((s, LANES), F32),
                   jax.ShapeDtypeStruct((MLA_HEADS, s, MLA_QK), BF16),
                   jax.ShapeDtypeStruct((MLA_HEADS, s, MLA_QK), BF16),
                   jax.ShapeDtypeStruct((MLA_HEADS, s, MLA_DV), BF16)],
        scratch_shapes=[pltpu.VMEM((tm + SUBLANES, GDN_WIDTH), F32),
                        pltpu.VMEM((SUBLANES, GDN_CONV_CH), F32)],
        compiler_params=_cparams(("arbitrary",)),
        name="even_in",
    )(x, mod3, g, w_in_p, conv_w, rope, qg, wuq_p, kvg, wukv_p)


GDN_SB = 256
GDN_TB = 512


def _gdn_kernel(qkv_ref, ba_ref, z_ref, alog_ref, dtb_ref, gn_ref, o_ref, s_ref):
    tb = GDN_SB
    nch = tb // CHUNK
    subs = range(GDN_TB // GDN_SB)

    @pl.when(pl.program_id(0) == 0)
    def _():
        s_ref[...] = jnp.zeros_like(s_ref)

    lane = lax.broadcasted_iota(jnp.int32, (tb, LANES), 1)
    ri = lax.broadcasted_iota(jnp.int32, (tb, tb), 0)
    ci = lax.broadcasted_iota(jnp.int32, (tb, tb), 1)
    same = (ri // CHUNK) == (ci // CHUNK)
    beta_all, gc_all, gc_t = [], [], []
    for b in subs:
        ba = ba_ref[b * tb:(b + 1) * tb, :]
        beta_all.append(jax.nn.sigmoid(ba))
        g_all = jnp.where((lane >= GDN_HEADS) & (lane < 2 * GDN_HEADS),
                          -jnp.exp(alog_ref[...]) * _softplus(ba + dtb_ref[...]), 0.0)
        gc_all.append(_mask_dot(same & (ci <= ri), g_all))
        gc_t.append(gc_all[b].T)
    gn = gn_ref[...]
    bd_mask = jnp.where(same, 1.0, 0.0).astype(BF16)

    wi = lax.broadcasted_iota(jnp.int32, (CHUNK, tb), 0)
    wl = lax.broadcasted_iota(jnp.int32, (CHUNK, tb), 1)
    wj, wc = wl % CHUNK, wl // CHUNK
    w_lower, w_strict = wj <= wi, wj < wi
    w_eye = jnp.where(wj == wi, 1.0, 0.0)

    def fold(full):
        out = full[0:CHUNK]
        for c in range(1, nch):
            out = jnp.where(wc == c, full[c * CHUNK:(c + 1) * CHUNK], out)
        return out

    def block_diag(xb):
        return jnp.concatenate([xb] * nch, axis=0) * bd_mask

    def split(x):
        hi = x.astype(BF16)
        return hi, (x - hi.astype(F32)).astype(BF16)

    def wide_dot3(xs, b):
        bh, bl = split(b)
        parts = [split(x) for x in xs]
        r = jnp.dot(jnp.concatenate([t for hl in parts for t in hl], axis=0), block_diag(bh),
                    preferred_element_type=F32)
        r2 = jnp.dot(jnp.concatenate([hl[0] for hl in parts], axis=0), block_diag(bl),
                     preferred_element_type=F32)
        return [r[2 * i * CHUNK:(2 * i + 1) * CHUNK] + r[(2 * i + 1) * CHUNK:(2 * i + 2) * CHUNK]
                + r2[i * CHUNK:(i + 1) * CHUNK] for i in range(len(xs))]

    heads = range(GDN_HEADS)
    units = [(b, hd) for b in subs for hd in heads]
    hv = []
    for b, hd in units:
        rows = slice(b * tb, (b + 1) * tb)
        gcol = gc_all[b][:, GDN_HEADS + hd:GDN_HEADS + hd + 1]
        grow = gc_t[b][GDN_HEADS + hd:GDN_HEADS + hd + 1, :]
        gcol_w = fold(jnp.broadcast_to(gcol, (tb, tb)))
        decay = jnp.where(w_lower, jnp.exp(jnp.where(w_lower, gcol_w - grow, 0.0)), 0.0)
        beta = beta_all[b][:, hd:hd + 1]
        q = qkv_ref[rows, hd * GDN_DK:(hd + 1) * GDN_DK]
        k = qkv_ref[rows, GDN_WIDTH + hd * GDN_DK:GDN_WIDTH + (hd + 1) * GDN_DK]
        v = qkv_ref[rows, 2 * GDN_WIDTH + hd * GDN_DV:2 * GDN_WIDTH + (hd + 1) * GDN_DV]
        kb = k * beta
        egc = jnp.exp(gcol)
        gl = jnp.concatenate([jnp.broadcast_to(gcol[(c + 1) * CHUNK - 1:(c + 1) * CHUNK, :], (CHUNK, 1))
                              for c in range(nch)], axis=0)
        hv.append(dict(decay=decay, q=q, k=k, kb=kb, qs=q * egc, egl=jnp.exp(gl),
                       rhs=jnp.concatenate([v * beta, kb * egc], axis=1).astype(BF16),
                       kend_t=(k * jnp.exp(gl - gcol)).T))

    pw = [jnp.where(w_strict, -(fold(_bdot_nt(h["kb"], h["k"])) * h["decay"]), 0.0) for h in hv]
    tw = [w_eye + p for p in pw]
    pw = [wide_dot3([p], p)[0] for p in pw]
    for _ in range(4):
        res = [wide_dot3([t, p], p) for t, p in zip(tw, pw)]
        tw = [t + r[0] for t, r in zip(tw, res)]
        pw = [r[1] for r in res]
    tw = [t + wide_dot3([t], p)[0] for t, p in zip(tw, pw)]
    pre = {}
    for i, unit in enumerate(units):
        h = hv[i]
        sol = jnp.dot(block_diag(tw[i].astype(BF16)), h["rhs"], preferred_element_type=F32)
        attn = fold(_bdot_nt(h["q"], h["k"])) * h["decay"]
        pre[unit] = (sol[:, 0:GDN_DV], sol[:, GDN_DV:], attn, h["qs"], h["kend_t"], h["egl"])

    col_chunk = lax.broadcasted_iota(jnp.int32, (GDN_DK, tb), 1) // CHUNK
    states = [s_ref[hd] for hd in heads]
    for b in subs:
        for c in range(nch):
            rs = slice(c * CHUNK, (c + 1) * CHUNK)
            pc = [pre[(b, hd)] for hd in heads]
            wss = [_bdot(jnp.concatenate([pc[hd][1][rs], pc[hd][3][rs]], axis=0), states[hd]) for hd in heads]
            v_full = []
            for hd in heads:
                parts = [jnp.zeros((CHUNK, GDN_DV), F32)] * nch
                parts[c] = pc[hd][0][rs] - wss[hd][0:CHUNK]
                v_full.append(jnp.concatenate(parts, axis=0))
            states = [states[hd] * pc[hd][5][c * CHUNK:c * CHUNK + 1, :]
                      + _bdot(jnp.where(col_chunk == c, pc[hd][4], 0.0), v_full[hd]) for hd in heads]
            out_rows = slice(b * tb + c * CHUNK, b * tb + (c + 1) * CHUNK)
            for hd in heads:
                o = wss[hd][CHUNK:] + _bdot(jnp.where(wc == c, pc[hd][2], 0.0), v_full[hd])
                ms = jnp.mean(o * o, axis=-1, keepdims=True)
                zz = z_ref[out_rows, hd * GDN_DV:(hd + 1) * GDN_DV]
                o_ref[out_rows, hd * GDN_DV:(hd + 1) * GDN_DV] = (
                    o * lax.rsqrt(ms + NORM_EPS) * gn * _silu(zz)).astype(BF16)
    for hd in heads:
        s_ref[hd] = states[hd]


def _gdn(qkv, ba, z, alog_row, dtb_row, gn):
    s = qkv.shape[0]
    tb = GDN_TB
    const = lambda i: (0, 0)
    row = lambda i: (i, 0)
    return pl.pallas_call(
        _gdn_kernel,
        grid=(s // tb,),
        in_specs=[pl.BlockSpec((tb, GDN_CONV_CH), row),
                  pl.BlockSpec((tb, LANES), row),
                  pl.BlockSpec((tb, GDN_WIDTH), row),
                  pl.BlockSpec((1, LANES), const),
                  pl.BlockSpec((1, LANES), const),
                  pl.BlockSpec((1, GDN_DV), const)],
        out_specs=pl.BlockSpec((tb, GDN_WIDTH), row),
        out_shape=jax.ShapeDtypeStruct((s, GDN_WIDTH), BF16),
        scratch_shapes=[pltpu.VMEM((GDN_HEADS, GDN_DK, GDN_DV), F32)],
        compiler_params=_cparams(("arbitrary",)),
        name="gdn",
    )(qkv, ba, z, alog_row, dtb_row, gn)


def _flash_kernel(q_ref, k_ref, v_ref, o_ref, m_ref, l_ref, acc_ref):
    t = FLASH_T
    qi = pl.program_id(1)
    nrep = t // LANES
    m_ref[...] = jnp.full_like(m_ref, -jnp.inf)
    l_ref[...] = jnp.zeros_like(l_ref)
    acc_ref[...] = jnp.zeros_like(acc_ref)

    def attend(kb, halves):
        start = pl.multiple_of(kb * t, t)
        k = k_ref[0, pl.ds(start, t), :]
        v = v_ref[0, pl.ds(start, t), :]
        scores = []
        for half, diagonal in halves:
            q = q_ref[0, half * t:(half + 1) * t, :]
            s = lax.dot_general(q, k, (((1,), (1,)), ((), ())), preferred_element_type=F32)
            if diagonal:
                qc = lax.broadcasted_iota(jnp.int32, (t, t), 0) // CHUNK
                kc = lax.broadcasted_iota(jnp.int32, (t, t), 1) // CHUNK
                s = jnp.where(kc <= qc, s, -jnp.inf)
            scores.append(s)
        probs = []
        for (half, _), s in zip(halves, scores):
            m_old = m_ref[half]
            m_new = jnp.maximum(m_old, jnp.max(s, axis=-1, keepdims=True))
            alpha = jnp.exp2(m_old - m_new)
            p = jnp.exp2(s - jnp.concatenate([m_new] * nrep, axis=1))
            psum = p[:, 0:LANES]
            for r in range(1, nrep):
                psum = psum + p[:, r * LANES:(r + 1) * LANES]
            l_ref[half] = l_ref[half] * alpha + psum
            m_ref[half] = m_new
            probs.append((alpha, p.astype(BF16)))
        for (half, _), (alpha, p) in zip(halves, probs):
            acc_ref[half] = acc_ref[half] * alpha + jnp.dot(p, v, preferred_element_type=F32)

    def body(i, carry):
        for kb in range(4):
            attend(4 * i + kb, [(0, False), (1, False)])
        return carry

    lax.fori_loop(0, qi // 2, body, 0)

    @pl.when(qi % 2 == 1)
    def _():
        for kb in (2 * qi - 2, 2 * qi - 1):
            attend(kb, [(0, False), (1, False)])

    attend(2 * qi, [(0, True), (1, False)])
    attend(2 * qi + 1, [(1, True)])
    for half in range(2):
        l = jnp.sum(l_ref[half], axis=-1, keepdims=True)
        o_ref[half * t:(half + 1) * t, :] = (acc_ref[half] / l).astype(o_ref.dtype)


def _flash(q, k, v):
    nh, s, _ = q.shape
    t = FLASH_T
    assert s % (2 * t) == 0
    return pl.pallas_call(
        _flash_kernel,
        grid=(nh, s // (2 * t)),
        in_specs=[pl.BlockSpec((1, 2 * t, MLA_QK), lambda h, qi: (h, qi, 0)),
                  pl.BlockSpec((1, s, MLA_QK), lambda h, qi: (h, 0, 0)),
                  pl.BlockSpec((1, s, MLA_DV), lambda h, qi: (h, 0, 0))],
        out_specs=pl.BlockSpec((2 * t, MLA_DV), lambda h, qi: (qi, h)),
        out_shape=jax.ShapeDtypeStruct((s, MLA_WIDTH), BF16),
        scratch_shapes=[pltpu.VMEM((2, t, LANES), F32), pltpu.VMEM((2, t, LANES), F32),
                        pltpu.VMEM((2, t, MLA_DV), F32)],
        compiler_params=_cparams(("parallel", "arbitrary")),
        name="mla_flash",
    )(q, k, v)


def _odd_in_kernel(x_ref, mod_ref, g_ref, w_ref, cw_ref, cb_ref, z_ref, xbc_ref, dt_ref, ext_ref, halo_ref, *, tc):
    @pl.when(pl.program_id(0) == 0)
    def _():
        halo_ref[...] = jnp.zeros_like(halo_ref)

    x = x_ref[...]
    h = _adaln(x, g_ref[...], mod_ref[0:1, :], mod_ref[1:2, :]).astype(BF16)
    proj = lambda c0, w: jnp.dot(h, w_ref[:, c0:c0 + w], preferred_element_type=F32)

    def finish(kind, c0, p):
        if kind == "z":
            z_ref[:, c0:c0 + tc] = p
        elif kind == "dt":
            dt_ref[...] = p[:, LANES:LANES + SSD_HEADS]
        else:
            y = _causal_conv(p, cw_ref[:, c0:c0 + tc], ext_ref, halo_ref, c0)
            xbc_ref[:, c0:c0 + tc] = _silu(y + cb_ref[:, c0:c0 + tc])

    nz, nx = SSD_D_INNER // tc, SSD_CONV_CH // tc
    tasks = []
    for j in range(max(nz, nx)):
        if j < nx:
            tasks.append(("xbc", j * tc, OD_XBC + j * tc, tc))
        if j < nz:
            tasks.append(("z", j * tc, OD_Z + j * tc, tc))
    tasks.append(("dt", 0, OD_DT, LANES + SSD_HEADS))
    pending = None
    for kind, c0, wc0, width in tasks:
        p = proj(wc0, width)
        if pending is not None:
            finish(*pending)
        pending = (kind, c0, p)
    finish(*pending)


def _odd_in(x, mod3, g, w_in, conv_w, conv_b):
    s = x.shape[0]
    tm = min(s, 512)
    tc = 512
    const = lambda i: (0, 0)
    row = lambda i: (i, 0)
    return pl.pallas_call(
        functools.partial(_odd_in_kernel, tc=tc),
        grid=(s // tm,),
        in_specs=[pl.BlockSpec((tm, D_MODEL), row),
                  pl.BlockSpec((3, D_MODEL), const),
                  pl.BlockSpec((1, D_MODEL), const),
                  pl.BlockSpec((D_MODEL, OD_IN), const),
                  pl.BlockSpec((CONV_K, SSD_CONV_CH), const),
                  pl.BlockSpec((1, SSD_CONV_CH), const)],
        out_specs=[pl.BlockSpec((tm, SSD_D_INNER), row),
                   pl.BlockSpec((tm, SSD_CONV_CH), row),
                   pl.BlockSpec((tm, SSD_HEADS), row)],
        out_shape=[jax.ShapeDtypeStruct((s, SSD_D_INNER), F32),
                   jax.ShapeDtypeStruct((s, SSD_CONV_CH), F32),
                   jax.ShapeDtypeStruct((s, SSD_HEADS), F32)],
        scratch_shapes=[pltpu.VMEM((tm + SUBLANES, tc), F32),
                        pltpu.VMEM((SUBLANES, SSD_CONV_CH), F32)],
        compiler_params=_cparams(("arbitrary",)),
        name="odd_in",
    )(x, mod3, g, w_in, conv_w, conv_b)


SSD_TB = 256
SSD_CHUNK = 128


def _ssd_kernel(xbc_ref, dt_ref, z_ref, alog_ref, dtb_ref, dskip_ref, ng_ref, o_ref, hs_ref):
    tb, lc = SSD_TB, SSD_CHUNK

    @pl.when(pl.program_id(0) == 0)
    def _():
        hs_ref[...] = jnp.zeros_like(hs_ref)

    lane = lax.broadcasted_iota(jnp.int32, (tb, LANES), 1)
    dt_raw = jnp.concatenate([dt_ref[...], jnp.zeros((tb, LANES - SSD_HEADS), F32)], axis=1)
    dt = jnp.where(lane < SSD_HEADS, _softplus(dt_raw + dtb_ref[...]), 0.0)
    da = dt * (-jnp.exp(alog_ref[...]))
    ri = lax.broadcasted_iota(jnp.int32, (tb, tb), 0)
    ci = lax.broadcasted_iota(jnp.int32, (tb, tb), 1)
    acs = _mask_dot(((ri // lc) == (ci // lc)) & (ci <= ri), da)
    acs = acs * math.log2(math.e)
    acs_t = acs.T
    ldt_t = jnp.log2(jnp.where(lane < SSD_HEADS, dt, 1.0)).T
    left = lax.broadcasted_iota(jnp.int32, (lc, LANES), 1) < SSD_HEADDIM
    halves = (left, jnp.logical_not(left))
    causal = jnp.where(lax.broadcasted_iota(jnp.int32, (lc, lc), 1) <= lax.broadcasted_iota(jnp.int32, (lc, lc), 0),
                       0.0, -1e30)
    groups = range(SSD_GROUPS)
    pairs = range(SSD_HEADS_PER_GROUP // 2)

    def gcols(g):
        return slice(g * SSD_GROUP_W, (g + 1) * SSD_GROUP_W)

    def pcols(pr):
        return slice(pr * LANES, (pr + 1) * LANES)

    states = [[hs_ref[g, :, pcols(pr)] for pr in pairs] for g in groups]
    for sc in range(tb // lc):
        rs = slice(sc * lc, (sc + 1) * lc)
        acs_c, acs_tc, ldt_tc = acs[rs], acs_t[:, rs], ldt_t[:, rs]
        a_last = acs_c[lc - 1:lc, :]
        c_decay = jnp.exp2(a_last)
        xs = [xbc_ref[rs, gcols(g)] for g in groups]
        bg_t = [xbc_ref[rs, SSD_D_INNER + g * SSD_STATE:SSD_D_INNER + (g + 1) * SSD_STATE].T for g in groups]
        cg = [xbc_ref[rs, SSD_D_INNER + (SSD_GROUPS + g) * SSD_STATE:SSD_D_INNER + (SSD_GROUPS + g + 1) * SSD_STATE]
              for g in groups]
        cb = [lax.dot_general(cg[g].astype(BF16), bg_t[g].astype(BF16), (((1,), (0,)), ((), ())),
                              preferred_element_type=F32) for g in groups]
        ys = [[None] * len(pairs) for _ in groups]
        new_states = [[None] * len(pairs) for _ in groups]
        for pr in pairs:
            for g in groups:
                x_pair = xs[g][:, pcols(pr)]
                st = states[g][pr]
                acc = None
                upd = None
                for half in range(2):
                    hd = g * SSD_HEADS_PER_GROUP + 2 * pr + half
                    colb = jnp.broadcast_to(acs_c[:, hd:hd + 1], (lc, LANES))
                    lrow = ldt_tc[hd:hd + 1, :] - acs_tc[hd:hd + 1, :]
                    m1 = cb[g] * jnp.exp2(colb + (lrow + causal))
                    c1 = cg[g] * jnp.exp2(colb)
                    xh = jnp.where(halves[half], x_pair, 0.0).astype(BF16)
                    sh = jnp.where(halves[half], st, 0.0).astype(BF16)
                    t = jnp.dot(jnp.concatenate([m1, c1], axis=1).astype(BF16), jnp.concatenate([xh, sh], axis=0),
                                preferred_element_type=F32)
                    acc = t if acc is None else acc + t
                    wrow = jnp.exp2(a_last[:, hd:hd + 1] + lrow)
                    u = jnp.dot((bg_t[g] * wrow).astype(BF16), xh, preferred_element_type=F32)
                    upd = u if upd is None else upd + u
                ys[g][pr] = acc
                h0 = g * SSD_HEADS_PER_GROUP + 2 * pr
                cdec = jnp.where(left[0:1], c_decay[:, h0:h0 + 1], c_decay[:, h0 + 1:h0 + 2])
                new_states[g][pr] = st * cdec + upd
        states = new_states
        for g in groups:
            y = jnp.concatenate(ys[g], axis=1) + dskip_ref[:, gcols(g)] * xs[g]
            y = y * _silu(z_ref[rs, gcols(g)])
            ms = jnp.mean(y * y, axis=-1, keepdims=True)
            o_ref[rs, gcols(g)] = (y * lax.rsqrt(ms + NORM_EPS) * ng_ref[:, gcols(g)]).astype(BF16)
    for g in groups:
        for pr in pairs:
            hs_ref[g, :, pcols(pr)] = states[g][pr]


def _ssd(xbc, dt, z, alog_row, dtb_row, dskip_row, ng):
    s = xbc.shape[0]
    tb = SSD_TB
    const = lambda i: (0, 0)
    row = lambda i: (i, 0)
    return pl.pallas_call(
        _ssd_kernel,
        grid=(s // tb,),
        in_specs=[pl.BlockSpec((tb, SSD_CONV_CH), row),
                  pl.BlockSpec((tb, SSD_HEADS), row),
                  pl.BlockSpec((tb, SSD_D_INNER), row),
                  pl.BlockSpec((1, LANES), const),
                  pl.BlockSpec((1, LANES), const),
                  pl.BlockSpec((1, SSD_D_INNER), const),
                  pl.BlockSpec((1, SSD_D_INNER), const)],
        out_specs=pl.BlockSpec((tb, SSD_D_INNER), row),
        out_shape=jax.ShapeDtypeStruct((s, SSD_D_INNER), BF16),
        scratch_shapes=[pltpu.VMEM((SSD_GROUPS, SSD_STATE, SSD_GROUP_W), F32)],
        compiler_params=_cparams(("arbitrary",)),
        name="ssd",
    )(xbc, dt, z, alog_row, dtb_row, dskip_row, ng)


def _pad_lanes(v, lane0):
    return jnp.zeros((1, LANES), F32).at[0, lane0:lane0 + v.shape[0]].set(v.astype(F32))


def _swap_halves(w):
    half = w.shape[-1] // 2
    return jnp.concatenate([w[..., half:], w[..., :half]], axis=-1)


def _even_w_in(w):
    qkv, z, beta, a, cq, ckv, kr = jnp.split(
        w, [GDN_CONV_CH, GDN_CONV_CH + GDN_WIDTH, GDN_CONV_CH + GDN_WIDTH + GDN_HEADS,
            GDN_CONV_CH + GDN_WIDTH + 2 * GDN_HEADS, GDN_CONV_CH + GDN_WIDTH + 2 * GDN_HEADS + MLA_Q_RANK,
            GDN_CONV_CH + GDN_WIDTH + 2 * GDN_HEADS + MLA_Q_RANK + MLA_KV_RANK], axis=1)
    pad = jnp.zeros((w.shape[0], LANES - 2 * GDN_HEADS), w.dtype)
    return jnp.concatenate([qkv, z, cq, ckv, kr, _swap_halves(kr), beta, a, pad], axis=1).astype(BF16)


def _even_w_uq(w):
    pe = w[..., MLA_NOPE:]
    return jnp.concatenate([w, _swap_halves(pe)], axis=-1).reshape(MLA_Q_RANK, MLA_HEADS * 2 * LANES).astype(BF16)


def kernel(x, c, positions, ada_w, ada_b, norm_g, ffn_w1, ffn_w3, ffn_w2, ev_w_in, gdn_conv_w, gdn_A_log, gdn_dt_bias, gdn_norm_g, mla_q_norm_g, mla_w_uq, mla_kv_norm_g, mla_w_ukv, ev_w_out, ssd_w_in, ssd_conv_w, ssd_conv_b, ssd_A_log, ssd_dt_bias, ssd_D, ssd_norm_g, ssd_w_out, final_g):
    b, s, d = x.shape
    assert b == 1 and d == D_MODEL and s % (2 * FLASH_T) == 0
    xs = x.reshape(s, d)
    mod = _modulation(c.reshape(d, 1), ada_w, ada_b).reshape(DEPTH, 3, 3, d)
    rope = _rope_table(positions.reshape(s, 1))
    fg = final_g.reshape(1, d)
    wb = (ffn_w1[0, 0].astype(BF16), ffn_w3[0, 0].astype(BF16), ffn_w2[0, 0].astype(BF16))
    for l in range(DEPTH):
        last = l == DEPTH - 1
        if l % 2:
            side = [(ssd_w_in, l // 2, OD_IN), (ssd_w_out, l // 2, D_MODEL)]
        else:
            side = [(ev_w_in, l // 2, EV_IN_P), (ev_w_out, l // 2, D_MODEL)]
        xs, wb, side = _ffn(xs, mod[l, 0], norm_g[l, 0].reshape(1, d), wb, fg, False,
                            cast_next=(ffn_w1, ffn_w3, ffn_w2, l, 1), cast_side=side)
        g1 = norm_g[l, 1].reshape(1, d)
        if l % 2 == 0:
            e = l // 2
            w_in_b, w_out_b = side
            qkv, z, ba, q, k, v = _even_in(
                xs, mod[l, 1], g1, w_in_b, gdn_conv_w[e], rope,
                mla_q_norm_g[e].reshape(1, -1), _even_w_uq(mla_w_uq[e]),
                mla_kv_norm_g[e].reshape(1, -1),
                mla_w_ukv[e].reshape(MLA_KV_RANK, MLA_HEADS * 2 * LANES).astype(BF16))
            o_a = _gdn(qkv, ba, z, _pad_lanes(gdn_A_log[e], GDN_HEADS), _pad_lanes(gdn_dt_bias[e], GDN_HEADS),
                       gdn_norm_g[e].reshape(1, -1))
            o_b = _flash(q, k, v)
            mix = (mod[l, 1], [o_a, o_b], w_out_b)
        else:
            o = l // 2
            w_in_b, w_out_b = side
            z, xbc, dt = _odd_in(xs, mod[l, 1], g1, w_in_b, ssd_conv_w[o], ssd_conv_b[o].reshape(1, -1))
            y = _ssd(xbc, dt, z, _pad_lanes(ssd_A_log[o], 0), _pad_lanes(ssd_dt_bias[o], 0),
                     jnp.repeat(ssd_D[o].astype(F32), SSD_HEADDIM).reshape(1, -1), ssd_norm_g[o].reshape(1, -1))
            mix = (mod[l, 1], [y], w_out_b)
        if last:
            xs = _ffn(xs, mod[l, 2], norm_g[l, 2].reshape(1, d), wb, fg, True, mix=mix)
        else:
            xs, wb, _ = _ffn(xs, mod[l, 2], norm_g[l, 2].reshape(1, d), wb, fg, False, mix=mix,
                             cast_next=(ffn_w1, ffn_w3, ffn_w2, l + 1, 0))
    return xs.reshape(b, s, d)
```

```python
import functools
import math

import jax
import jax.numpy as jnp
from jax import lax
from jax.experimental import pallas as pl
from jax.experimental.pallas import tpu as pltpu

F32 = jnp.float32
BF16 = jnp.bfloat16

D_MODEL = 1024
DEPTH = 4
CHUNK = 64
NORM_EPS = 1e-6
CONV_K = 4
D_FF = 2816
N_MOD = 9

GDN_HEADS = 4
GDN_DK = 128
GDN_DV = 128
GDN_WIDTH = GDN_HEADS * GDN_DV
GDN_CONV_CH = 2 * GDN_HEADS * GDN_DK + GDN_WIDTH

MLA_HEADS = 4
MLA_Q_RANK = 384
MLA_KV_RANK = 256
MLA_NOPE = 128
MLA_ROPE = 64
MLA_DV = 128
MLA_WIDTH = MLA_HEADS * MLA_DV
MLA_QK = MLA_NOPE + MLA_ROPE
ROPE_THETA = 10000.0

SSD_D_INNER = 2 * D_MODEL
SSD_HEADDIM = 64
SSD_HEADS = SSD_D_INNER // SSD_HEADDIM
SSD_GROUPS = 4
SSD_STATE = 128
SSD_CONV_CH = SSD_D_INNER + 2 * SSD_GROUPS * SSD_STATE
SSD_GROUP_W = SSD_D_INNER // SSD_GROUPS
SSD_HEADS_PER_GROUP = SSD_HEADS // SSD_GROUPS

LANES = 128
SUBLANES = 8
VMEM_LIMIT = 56 * 1024 * 1024
FLASH_T = 512

EV_QKV = 0
EV_Z = GDN_CONV_CH
EV_CQ = EV_Z + GDN_WIDTH
EV_CKV = EV_CQ + MLA_Q_RANK
EV_KR = EV_CKV + MLA_KV_RANK
EV_BA = EV_KR + 2 * MLA_ROPE
EV_IN_P = EV_BA + LANES

OD_Z = 0
OD_XBC = SSD_D_INNER
OD_IN = OD_XBC + SSD_CONV_CH + SSD_HEADS
OD_DT = OD_IN - SSD_HEADS - LANES
OD_IN_CAST_STEPS = 23


def _cparams(sem):
    return pltpu.CompilerParams(dimension_semantics=sem, vmem_limit_bytes=VMEM_LIMIT)


def _bdot(a, b):
    return jnp.dot(a.astype(BF16), b.astype(BF16), preferred_element_type=F32)


def _bdot_nt(a, b):
    return lax.dot_general(a.astype(BF16), b.astype(BF16), (((1,), (1,)), ((), ())),
                           preferred_element_type=F32)


def _mask_dot(mask, x):
    m = jnp.where(mask, 1.0, 0.0).astype(BF16)
    out = None
    r = x
    for _ in range(3):
        part = r.astype(BF16)
        r = r - part.astype(F32)
        t = jnp.dot(m, part, preferred_element_type=F32)
        out = t if out is None else out + t
    return out


def _silu(x):
    hx = 0.5 * x
    return hx + hx * jnp.tanh(hx)


def _softplus(x):
    return jnp.maximum(x, 0.0) + jnp.log1p(jnp.exp(-jnp.abs(x)))


def _adaln(x, g, shift, scale):
    ms = jnp.mean(x * x, axis=-1, keepdims=True)
    y = x * lax.rsqrt(ms + NORM_EPS) * g
    return y * (1.0 + scale) + shift


def _mod_kernel(c_ref, w_ref, b_ref, o_ref):
    ca = _silu(c_ref[...])
    cb = jnp.broadcast_to(ca, (D_MODEL, LANES))
    tn = w_ref.shape[2]
    for t in range(tn // LANES):
        sl = slice(t * LANES, (t + 1) * LANES)
        o_ref[0, :, sl] = jnp.sum(w_ref[0, :, sl] * cb, axis=0, keepdims=True) + b_ref[0, :, sl]


def _modulation(c_col, ada_w, ada_b):
    n = N_MOD * D_MODEL
    tn = n // 4
    return pl.pallas_call(
        _mod_kernel,
        grid=(DEPTH, n // tn),
        in_specs=[pl.BlockSpec((D_MODEL, 1), lambda l, j: (0, 0)),
                  pl.BlockSpec((1, D_MODEL, tn), lambda l, j: (l, 0, j)),
                  pl.BlockSpec((1, 1, tn), lambda l, j: (l, 0, j))],
        out_specs=pl.BlockSpec((1, 1, tn), lambda l, j: (l, 0, j)),
        out_shape=jax.ShapeDtypeStruct((DEPTH, 1, n), F32),
        compiler_params=_cparams(("parallel", "parallel")),
        name="modulation",
    )(c_col, ada_w, ada_b.reshape(DEPTH, 1, n))


def _rope_kernel(pos_ref, freq_ref, o_ref):
    ang = pos_ref[...].astype(F32) * freq_ref[...]
    lane = lax.broadcasted_iota(jnp.int32, ang.shape, 1)
    sgn = jnp.where(lane < 3 * (MLA_ROPE // 2), -1.0, 1.0)
    o_ref[...] = jnp.where(lane < MLA_ROPE, jnp.cos(ang), sgn * jnp.sin(ang))


def _rope_table(pos_col):
    s = pos_col.shape[0]
    tm = min(s, 2048)
    half = MLA_ROPE // 2
    inv_freq = ROPE_THETA ** (-jnp.arange(half, dtype=F32) / half)
    freq = jnp.tile(inv_freq, LANES // half).reshape(1, LANES)
    return pl.pallas_call(
        _rope_kernel,
        grid=(s // tm,),
        in_specs=[pl.BlockSpec((tm, 1), lambda i: (i, 0)),
                  pl.BlockSpec((1, LANES), lambda i: (0, 0))],
        out_specs=pl.BlockSpec((tm, LANES), lambda i: (i, 0)),
        out_shape=jax.ShapeDtypeStruct((s, LANES), F32),
        compiler_params=_cparams(("parallel",)),
        name="rope_table",
    )(pos_col, freq)


def _ffn_kernel(*refs, tf, final, n_mix, cast_steps):
    x_ref, mod_ref, g_ref, w1_ref, w3_ref, w2_ref, fg_ref = refs[0:7]
    n_extra = n_mix + 2 if n_mix else 0
    mix_refs = refs[7:7 + n_extra]
    n_in = len(cast_steps)
    cast_in = refs[7 + n_extra:7 + n_extra + n_in]
    o_ref = refs[7 + n_extra + n_in]
    cast_out = refs[8 + n_extra + n_in:8 + n_extra + 2 * n_in]
    a_ref = refs[8 + n_extra + 2 * n_in]
    for steps, src, dst in zip(cast_steps, cast_in, cast_out):
        def cast(src=src, dst=dst):
            if src.shape[1] == dst.shape[1]:
                dst[...] = src[...].astype(BF16)
            else:
                dst[...] = _even_w_in(src[...])
        if steps is None:
            cast()
        else:
            pl.when(pl.program_id(0) < steps)(cast)
    x = x_ref[...]
    if n_mix:
        mmod_ref, w_out_ref = mix_refs[0], mix_refs[-1]
        y = None
        k0 = 0
        for act_ref in mix_refs[1:-1]:
            kw = act_ref.shape[1]
            t = jnp.dot(act_ref[...], w_out_ref[k0:k0 + kw, :], preferred_element_type=F32)
            y = t if y is None else y + t
            k0 += kw
        x = x + mmod_ref[2:3, :] * y
    h = _adaln(x, g_ref[...], mod_ref[0:1, :], mod_ref[1:2, :]).astype(BF16)
    for j in range(D_FF // tf):
        sl = slice(j * tf, (j + 1) * tf)
        h1 = jnp.dot(h, w1_ref[:, sl], preferred_element_type=F32)
        h3 = jnp.dot(h, w3_ref[:, sl], preferred_element_type=F32)
        a_ref[:, sl] = (_silu(h1) * h3).astype(BF16)
    y = jnp.dot(a_ref[...], w2_ref[...], preferred_element_type=F32)
    xn = x + (0.5 * mod_ref[2:3, :]) * y
    if final:
        ms = jnp.mean(xn * xn, axis=-1, keepdims=True)
        xn = xn * lax.rsqrt(ms + NORM_EPS) * fg_ref[...]
    o_ref[...] = xn


FFN_W2_CAST_STEPS = 16


def _ffn(x, mod3, g, wb, final_g, final, mix=None, cast_next=None, cast_side=()):
    s = x.shape[0]
    tm = min(s, 512)
    n = s // tm
    const = lambda i: (0, 0)
    row = lambda i: (i, 0)
    in_specs = [pl.BlockSpec((tm, D_MODEL), row),
                pl.BlockSpec((3, D_MODEL), const),
                pl.BlockSpec((1, D_MODEL), const),
                pl.BlockSpec((D_MODEL, D_FF), const, pipeline_mode=pl.Buffered(1)),
                pl.BlockSpec((D_MODEL, D_FF), const, pipeline_mode=pl.Buffered(1)),
                pl.BlockSpec((D_FF, D_MODEL), const, pipeline_mode=pl.Buffered(1)),
                pl.BlockSpec((1, D_MODEL), const)]
    args = [x, mod3, g, *wb, final_g]
    n_mix = 0
    if mix is not None:
        mmod, acts, w_out = mix
        n_mix = len(acts)
        in_specs += ([pl.BlockSpec((3, D_MODEL), const)]
                     + [pl.BlockSpec((tm, a.shape[1]), row) for a in acts]
                     + [pl.BlockSpec(w_out.shape, const, pipeline_mode=pl.Buffered(1))])
        args += [mmod, *acts, w_out]
    out_specs = [pl.BlockSpec((tm, D_MODEL), row)]
    out_shape = [jax.ShapeDtypeStruct((s, D_MODEL), F32)]
    cast_steps = []
    if cast_next is not None:
        nw1, nw3, nw2, layer, which = cast_next
        n2 = min(n, FFN_W2_CAST_STEPS)
        r1, r2 = D_MODEL // n, D_FF // n2
        slab = lambda i: (layer, which, i, 0)
        slab2 = lambda i: (layer, which, jnp.minimum(i, n2 - 1), 0)
        in_specs += [pl.BlockSpec((None, None, r1, D_FF), slab), pl.BlockSpec((None, None, r1, D_FF), slab),
                     pl.BlockSpec((None, None, r2, D_MODEL), slab2)]
        args += [nw1, nw3, nw2]
        out_specs += [pl.BlockSpec((r1, D_FF), row), pl.BlockSpec((r1, D_FF), row),
                      pl.BlockSpec((r2, D_MODEL), lambda i: (jnp.minimum(i, n2 - 1), 0))]
        out_shape += [jax.ShapeDtypeStruct((D_MODEL, D_FF), BF16), jax.ShapeDtypeStruct((D_MODEL, D_FF), BF16),
                      jax.ShapeDtypeStruct((D_FF, D_MODEL), BF16)]
        cast_steps += [None, None, n2 if n2 < n else None]
    for w, idx, out_cols, steps in cast_side:
        rows, cols = w.shape[1:]
        steps = n if steps is None else min(steps, n)
        last = steps - 1
        in_specs.append(pl.BlockSpec((None, rows // steps, cols), lambda i, idx=idx, last=last: (idx, jnp.minimum(i, last), 0)))
        args.append(w)
        out_specs.append(pl.BlockSpec((rows // steps, out_cols), lambda i, last=last: (jnp.minimum(i, last), 0)))
        out_shape.append(jax.ShapeDtypeStruct((rows, out_cols), BF16))
        cast_steps.append(steps if steps < n else None)
    out = pl.pallas_call(
        functools.partial(_ffn_kernel, tf=256, final=final, n_mix=n_mix, cast_steps=tuple(cast_steps)),
        grid=(n,),
        in_specs=in_specs,
        out_specs=out_specs,
        out_shape=out_shape,
        scratch_shapes=[pltpu.VMEM((tm, D_FF), BF16)],
        compiler_params=_cparams(("arbitrary",)),
        name="ffn_final" if final else ("mix_ffn" if n_mix else "ffn"),
    )(*args)
    if cast_next is None:
        return out[0]
    return out[0], tuple(out[1:4]), list(out[4:])


def _causal_conv(p, cw, ext_ref, halo_ref, col0):
    tm, w = p.shape
    cs = slice(col0, col0 + w)
    ext_ref[0:SUBLANES, 0:w] = halo_ref[:, cs]
    ext_ref[SUBLANES:SUBLANES + tm, 0:w] = p
    halo_ref[:, cs] = p[tm - SUBLANES:, :]
    y = p * cw[CONV_K - 1:CONV_K, :]
    for j in range(CONV_K - 1):
        off = SUBLANES - (CONV_K - 1) + j
        y = y + ext_ref[off:off + tm, 0:w] * cw[j:j + 1, :]
    return y


def _even_in_kernel(x_ref, mod_ref, g_ref, w_ref, cw_ref, rope_ref, qg_ref, wuq_ref, kvg_ref, wukv_ref,
                    qkv_ref, z_ref, ba_ref, q_ref, k_ref, v_ref, ext_ref, halo_ref):
    @pl.when(pl.program_id(0) == 0)
    def _():
        halo_ref[...] = jnp.zeros_like(halo_ref)

    x = x_ref[...]
    h = _adaln(x, g_ref[...], mod_ref[0:1, :], mod_ref[1:2, :]).astype(BF16)
    proj = lambda c0, w: jnp.dot(h, w_ref[:, c0:c0 + w], preferred_element_type=F32)

    rope = rope_ref[...]
    scale = MLA_QK ** -0.5 * math.log2(math.e)

    def rot(xx):
        yy = xx * rope
        return yy + pltpu.roll(yy, MLA_ROPE, 1)

    def rms(v, g):
        return (v * lax.rsqrt(jnp.mean(v * v, axis=-1, keepdims=True) + NORM_EPS) * g).astype(BF16)

    def gdn_part(part, p):
        c0 = part * GDN_WIDTH
        y = _silu(_causal_conv(p, cw_ref[:, c0:c0 + GDN_WIDTH], ext_ref, halo_ref, c0))
        if part == 2:
            qkv_ref[:, c0:c0 + GDN_WIDTH] = y
            return
        post = GDN_DK ** -0.5 if part == 0 else 1.0
        for hd in range(GDN_HEADS):
            yh = y[:, hd * GDN_DK:(hd + 1) * GDN_DK]
            nrm = lax.rsqrt(jnp.sum(yh * yh, axis=-1, keepdims=True) + NORM_EPS)
            qkv_ref[:, c0 + hd * GDN_DK:c0 + (hd + 1) * GDN_DK] = yh * nrm * post

    def mla_q(qf):
        for hd in range(MLA_HEADS):
            b0 = hd * 2 * LANES
            q_ref[hd, :, 0:MLA_NOPE] = (qf[:, b0:b0 + MLA_NOPE] * scale).astype(BF16)
            qpe = rot(qf[:, b0 + LANES:b0 + 2 * LANES]) * scale
            q_ref[hd, :, MLA_NOPE:MLA_QK] = qpe[:, 0:MLA_ROPE].astype(BF16)

    def mla_kv(kvf, kpe):
        for hd in range(MLA_HEADS):
            b0 = hd * 2 * LANES
            k_ref[hd, :, 0:MLA_NOPE] = kvf[:, b0:b0 + MLA_NOPE].astype(BF16)
            k_ref[hd, :, MLA_NOPE:MLA_QK] = kpe
            v_ref[hd] = kvf[:, b0 + LANES:b0 + 2 * LANES].astype(BF16)

    cq, ckv, kr = proj(EV_CQ, MLA_Q_RANK), proj(EV_CKV, MLA_KV_RANK), proj(EV_KR, 2 * MLA_ROPE)
    p0 = proj(0, GDN_WIDTH)
    cqn, ckvn = rms(cq, qg_ref[...]), rms(ckv, kvg_ref[...])
    kpe = rot(kr)[:, 0:MLA_ROPE].astype(BF16)
    qf = jnp.dot(cqn, wuq_ref[...], preferred_element_type=F32)
    gdn_part(0, p0)
    p1 = proj(GDN_WIDTH, GDN_WIDTH)
    mla_q(qf)
    kvf = jnp.dot(ckvn, wukv_ref[...], preferred_element_type=F32)
    gdn_part(1, p1)
    p2 = proj(2 * GDN_WIDTH, GDN_WIDTH)
    mla_kv(kvf, kpe)
    zb = proj(EV_Z, GDN_WIDTH), proj(EV_BA, LANES)
    gdn_part(2, p2)
    z_ref[...], ba_ref[...] = zb


def _even_in(x, mod3, g, w_in_p, conv_w, rope, qg, wuq_p, kvg, wukv_p):
    s = x.shape[0]
    tm = min(s, 512)
    const = lambda i: (0, 0)
    row = lambda i: (i, 0)
    hrow = lambda i: (0, i, 0)
    return pl.pallas_call(
        _even_in_kernel,
        grid=(s // tm,),
        in_specs=[pl.BlockSpec((tm, D_MODEL), row),
                  pl.BlockSpec((3, D_MODEL), const),
                  pl.BlockSpec((1, D_MODEL), const),
                  pl.BlockSpec((D_MODEL, EV_IN_P), const),
                  pl.BlockSpec((CONV_K, GDN_CONV_CH), const),
                  pl.BlockSpec((tm, LANES), row),
                  pl.BlockSpec((1, MLA_Q_RANK), const),
                  pl.BlockSpec((MLA_Q_RANK, MLA_HEADS * 2 * LANES), const),
                  pl.BlockSpec((1, MLA_KV_RANK), const),
                  pl.BlockSpec((MLA_KV_RANK, MLA_HEADS * 2 * LANES), const)],
        out_specs=[pl.BlockSpec((tm, GDN_CONV_CH), row),
                   pl.BlockSpec((tm, GDN_WIDTH), row),
                   pl.BlockSpec((tm, LANES), row),
                   pl.BlockSpec((MLA_HEADS, tm, MLA_QK), hrow),
                   pl.BlockSpec((MLA_HEADS, tm, MLA_QK), hrow),
                   pl.BlockSpec((MLA_HEADS, tm, MLA_DV), hrow)],
        out_shape=[jax.ShapeDtypeStruct((s, GDN_CONV_CH), F32),
                   jax.ShapeDtypeStruct((s, GDN_WIDTH), F32),
                   jax.ShapeDtypeStruct((s, LANES), F32),
                   jax.ShapeDtypeStruct((MLA_HEADS, s, MLA_QK), BF16),
                   jax.ShapeDtypeStruct((MLA_HEADS, s, MLA_QK), BF16),
                   jax.ShapeDtypeStruct((MLA_HEADS, s, MLA_DV), BF16)],
        scratch_shapes=[pltpu.VMEM((tm + SUBLANES, GDN_WIDTH), F32),
                        pltpu.VMEM((SUBLANES, GDN_CONV_CH), F32)],
        compiler_params=_cparams(("arbitrary",)),
        name="even_in",
    )(x, mod3, g, w_in_p, conv_w, rope, qg, wuq_p, kvg, wukv_p)


GDN_SB = 256
GDN_TB = 512


def _gdn_kernel(qkv_ref, ba_ref, z_ref, alog_ref, dtb_ref, gn_ref, o_ref, s_ref):
    tb = GDN_SB
    nch = tb // CHUNK
    subs = range(GDN_TB // GDN_SB)

    @pl.when(pl.program_id(0) == 0)
    def _():
        s_ref[...] = jnp.zeros_like(s_ref)

    lane = lax.broadcasted_iota(jnp.int32, (tb, LANES), 1)
    ri = lax.broadcasted_iota(jnp.int32, (tb, tb), 0)
    ci = lax.broadcasted_iota(jnp.int32, (tb, tb), 1)
    same = (ri // CHUNK) == (ci // CHUNK)
    beta_all, gc_all, gc_t = [], [], []
    for b in subs:
        ba = ba_ref[b * tb:(b + 1) * tb, :]
        beta_all.append(jax.nn.sigmoid(ba))
        g_all = jnp.where((lane >= GDN_HEADS) & (lane < 2 * GDN_HEADS),
                          -jnp.exp(alog_ref[...]) * _softplus(ba + dtb_ref[...]), 0.0)
        gc_all.append(_mask_dot(same & (ci <= ri), g_all))
        gc_t.append(gc_all[b].T)
    gn = gn_ref[...]
    bd_mask = jnp.where(same, 1.0, 0.0).astype(BF16)

    wi = lax.broadcasted_iota(jnp.int32, (CHUNK, tb), 0)
    wl = lax.broadcasted_iota(jnp.int32, (CHUNK, tb), 1)
    wj, wc = wl % CHUNK, wl // CHUNK
    w_lower, w_strict = wj <= wi, wj < wi
    w_eye = jnp.where(wj == wi, 1.0, 0.0)

    def fold(full):
        out = full[0:CHUNK]
        for c in range(1, nch):
            out = jnp.where(wc == c, full[c * CHUNK:(c + 1) * CHUNK], out)
        return out

    def block_diag(xb):
        return jnp.concatenate([xb] * nch, axis=0) * bd_mask

    def split(x):
        hi = x.astype(BF16)
        return hi, (x - hi.astype(F32)).astype(BF16)

    def wide_dot3(xs, b):
        bh, bl = split(b)
        parts = [split(x) for x in xs]
        r = jnp.dot(jnp.concatenate([t for hl in parts for t in hl], axis=0), block_diag(bh),
                    preferred_element_type=F32)
        r2 = jnp.dot(jnp.concatenate([hl[0] for hl in parts], axis=0), block_diag(bl),
                     preferred_element_type=F32)
        return [r[2 * i * CHUNK:(2 * i + 1) * CHUNK] + r[(2 * i + 1) * CHUNK:(2 * i + 2) * CHUNK]
                + r2[i * CHUNK:(i + 1) * CHUNK] for i in range(len(xs))]

    heads = range(GDN_HEADS)
    units = [(b, hd) for b in subs for hd in heads]
    hv = []
    for b, hd in units:
        rows = slice(b * tb, (b + 1) * tb)
        gcol = gc_all[b][:, GDN_HEADS + hd:GDN_HEADS + hd + 1]
        grow = gc_t[b][GDN_HEADS + hd:GDN_HEADS + hd + 1, :]
        gcol_w = fold(jnp.broadcast_to(gcol, (tb, tb)))
        decay = jnp.where(w_lower, jnp.exp(jnp.where(w_lower, gcol_w - grow, 0.0)), 0.0)
        beta = beta_all[b][:, hd:hd + 1]
        q = qkv_ref[rows, hd * GDN_DK:(hd + 1) * GDN_DK]
        k = qkv_ref[rows, GDN_WIDTH + hd * GDN_DK:GDN_WIDTH + (hd + 1) * GDN_DK]
        v = qkv_ref[rows, 2 * GDN_WIDTH + hd * GDN_DV:2 * GDN_WIDTH + (hd + 1) * GDN_DV]
        kb = k * beta
        egc = jnp.exp(gcol)
        gl = jnp.concatenate([jnp.broadcast_to(gcol[(c + 1) * CHUNK - 1:(c + 1) * CHUNK, :], (CHUNK, 1))
                              for c in range(nch)], axis=0)
        hv.append(dict(decay=decay, q=q, k=k, kb=kb, qs=q * egc, egl=jnp.exp(gl),
                       rhs=jnp.concatenate([v * beta, kb * egc], axis=1).astype(BF16),
                       kend_t=(k * jnp.exp(gl - gcol)).T))

    pw = [jnp.where(w_strict, -(fold(_bdot_nt(h["kb"], h["k"])) * h["decay"]), 0.0) for h in hv]
    tw = [w_eye + p for p in pw]
    pw = [wide_dot3([p], p)[0] for p in pw]
    for _ in range(4):
        res = [wide_dot3([t, p], p) for t, p in zip(tw, pw)]
        tw = [t + r[0] for t, r in zip(tw, res)]
        pw = [r[1] for r in res]
    tw = [t + wide_dot3([t], p)[0] for t, p in zip(tw, pw)]
    pre = {}
    for i, unit in enumerate(units):
        h = hv[i]
        sol = jnp.dot(block_diag(tw[i].astype(BF16)), h["rhs"], preferred_element_type=F32)
        attn = fold(_bdot_nt(h["q"], h["k"])) * h["decay"]
        pre[unit] = (sol[:, 0:GDN_DV], sol[:, GDN_DV:], attn, h["qs"], h["kend_t"], h["egl"])

    col_chunk = lax.broadcasted_iota(jnp.int32, (GDN_DK, tb), 1) // CHUNK
    states = [s_ref[hd] for hd in heads]
    for b in subs:
        for c in range(nch):
            rs = slice(c * CHUNK, (c + 1) * CHUNK)
            pc = [pre[(b, hd)] for hd in heads]
            wss = [_bdot(jnp.concatenate([pc[hd][1][rs], pc[hd][3][rs]], axis=0), states[hd]) for hd in heads]
            v_full = []
            for hd in heads:
                parts = [jnp.zeros((CHUNK, GDN_DV), F32)] * nch
                parts[c] = pc[hd][0][rs] - wss[hd][0:CHUNK]
                v_full.append(jnp.concatenate(parts, axis=0))
            states = [states[hd] * pc[hd][5][c * CHUNK:c * CHUNK + 1, :]
                      + _bdot(jnp.where(col_chunk == c, pc[hd][4], 0.0), v_full[hd]) for hd in heads]
            out_rows = slice(b * tb + c * CHUNK, b * tb + (c + 1) * CHUNK)
            for hd in heads:
                o = wss[hd][CHUNK:] + _bdot(jnp.where(wc == c, pc[hd][2], 0.0), v_full[hd])
                ms = jnp.mean(o * o, axis=-1, keepdims=True)
                zz = z_ref[out_rows, hd * GDN_DV:(hd + 1) * GDN_DV]
                o_ref[out_rows, hd * GDN_DV:(hd + 1) * GDN_DV] = (
                    o * lax.rsqrt(ms + NORM_EPS) * gn * _silu(zz)).astype(BF16)
    for hd in heads:
        s_ref[hd] = states[hd]


def _gdn(qkv, ba, z, alog_row, dtb_row, gn):
    s = qkv.shape[0]
    tb = GDN_TB
    const = lambda i: (0, 0)
    row = lambda i: (i, 0)
    return pl.pallas_call(
        _gdn_kernel,
        grid=(s // tb,),
        in_specs=[pl.BlockSpec((tb, GDN_CONV_CH), row),
                  pl.BlockSpec((tb, LANES), row),
                  pl.BlockSpec((tb, GDN_WIDTH), row),
                  pl.BlockSpec((1, LANES), const),
                  pl.BlockSpec((1, LANES), const),
                  pl.BlockSpec((1, GDN_DV), const)],
        out_specs=pl.BlockSpec((tb, GDN_WIDTH), row),
        out_shape=jax.ShapeDtypeStruct((s, GDN_WIDTH), BF16),
        scratch_shapes=[pltpu.VMEM((GDN_HEADS, GDN_DK, GDN_DV), F32)],
        compiler_params=_cparams(("arbitrary",)),
        name="gdn",
    )(qkv, ba, z, alog_row, dtb_row, gn)


def _flash_kernel(q_ref, k_ref, v_ref, o_ref, m_ref, l_ref, acc_ref):
    t = FLASH_T
    qi = pl.program_id(1)
    nrep = t // LANES
    m_ref[...] = jnp.full_like(m_ref, -jnp.inf)
    l_ref[...] = jnp.zeros_like(l_ref)
    acc_ref[...] = jnp.zeros_like(acc_ref)

    def attend(kb, halves):
        start = pl.multiple_of(kb * t, t)
        k = k_ref[0, pl.ds(start, t), :]
        v = v_ref[0, pl.ds(start, t), :]
        scores = []
        for half, diagonal in halves:
            q = q_ref[0, half * t:(half + 1) * t, :]
            s = lax.dot_general(q, k, (((1,), (1,)), ((), ())), preferred_element_type=F32)
            if diagonal:
                qc = lax.broadcasted_iota(jnp.int32, (t, t), 0) // CHUNK
                kc = lax.broadcasted_iota(jnp.int32, (t, t), 1) // CHUNK
                s = jnp.where(kc <= qc, s, -jnp.inf)
            scores.append(s)
        probs = []
        for (half, _), s in zip(halves, scores):
            m_old = m_ref[half]
            m_new = jnp.maximum(m_old, jnp.max(s, axis=-1, keepdims=True))
            alpha = jnp.exp2(m_old - m_new)
            p = jnp.exp2(s - jnp.concatenate([m_new] * nrep, axis=1))
            psum = p[:, 0:LANES]
            for r in range(1, nrep):
                psum = psum + p[:, r * LANES:(r + 1) * LANES]
            l_ref[half] = l_ref[half] * alpha + psum
            m_ref[half] = m_new
            probs.append((alpha, p.astype(BF16)))
        for (half, _), (alpha, p) in zip(halves, probs):
            acc_ref[half] = acc_ref[half] * alpha + jnp.dot(p, v, preferred_element_type=F32)

    def body(i, carry):
        for kb in range(4):
            attend(4 * i + kb, [(0, False), (1, False)])
        return carry

    lax.fori_loop(0, qi // 2, body, 0)

    @pl.when(qi % 2 == 1)
    def _():
        for kb in (2 * qi - 2, 2 * qi - 1):
            attend(kb, [(0, False), (1, False)])

    attend(2 * qi, [(0, True), (1, False)])
    attend(2 * qi + 1, [(1, True)])
    for half in range(2):
        l = jnp.sum(l_ref[half], axis=-1, keepdims=True)
        o_ref[half * t:(half + 1) * t, :] = (acc_ref[half] / l).astype(o_ref.dtype)


def _flash(q, k, v):
    nh, s, _ = q.shape
    t = FLASH_T
    assert s % (2 * t) == 0
    return pl.pallas_call(
        _flash_kernel,
        grid=(nh, s // (2 * t)),
        in_specs=[pl.BlockSpec((1, 2 * t, MLA_QK), lambda h, qi: (h, qi, 0)),
                  pl.BlockSpec((1, s, MLA_QK), lambda h, qi: (h, 0, 0)),
                  pl.BlockSpec((1, s, MLA_DV), lambda h, qi: (h, 0, 0))],
        out_specs=pl.BlockSpec((2 * t, MLA_DV), lambda h, qi: (qi, h)),
        out_shape=jax.ShapeDtypeStruct((s, MLA_WIDTH), BF16),
        scratch_shapes=[pltpu.VMEM((2, t, LANES), F32), pltpu.VMEM((2, t, LANES), F32),
                        pltpu.VMEM((2, t, MLA_DV), F32)],
        compiler_params=_cparams(("parallel", "arbitrary")),
        name="mla_flash",
    )(q, k, v)


def _odd_in_kernel(x_ref, mod_ref, g_ref, w_ref, cw_ref, cb_ref, z_ref, xbc_ref, dt_ref, ext_ref, halo_ref, *, tc):
    @pl.when(pl.program_id(0) == 0)
    def _():
        halo_ref[...] = jnp.zeros_like(halo_ref)

    x = x_ref[...]
    h = _adaln(x, g_ref[...], mod_ref[0:1, :], mod_ref[1:2, :]).astype(BF16)
    proj = lambda c0, w: lax.dot_general(h, w_ref[c0:c0 + w, :], (((1,), (1,)), ((), ())),
                                         preferred_element_type=F32)

    def finish(kind, c0, p):
        if kind == "z":
            z_ref[:, c0:c0 + tc] = p
        elif kind == "dt":
            dt_ref[...] = p[:, LANES:LANES + SSD_HEADS]
        else:
            y = _causal_conv(p, cw_ref[:, c0:c0 + tc], ext_ref, halo_ref, c0)
            xbc_ref[:, c0:c0 + tc] = _silu(y + cb_ref[:, c0:c0 + tc])

    nz, nx = SSD_D_INNER // tc, SSD_CONV_CH // tc
    tasks = []
    for j in range(max(nz, nx)):
        if j < nx:
            tasks.append(("xbc", j * tc, OD_XBC + j * tc, tc))
        if j < nz:
            tasks.append(("z", j * tc, OD_Z + j * tc, tc))
    tasks.append(("dt", 0, OD_DT, LANES + SSD_HEADS))
    pending = None
    for kind, c0, wc0, width in tasks:
        p = proj(wc0, width)
        if pending is not None:
            finish(*pending)
        pending = (kind, c0, p)
    finish(*pending)


def _odd_in(x, mod3, g, w_in_t, conv_w, conv_b):
    s = x.shape[0]
    tm = min(s, 512)
    tc = 512
    const = lambda i: (0, 0)
    row = lambda i: (i, 0)
    return pl.pallas_call(
        functools.partial(_odd_in_kernel, tc=tc),
        grid=(s // tm,),
        in_specs=[pl.BlockSpec((tm, D_MODEL), row),
                  pl.BlockSpec((3, D_MODEL), const),
                  pl.BlockSpec((1, D_MODEL), const),
                  pl.BlockSpec((OD_IN, D_MODEL), const),
                  pl.BlockSpec((CONV_K, SSD_CONV_CH), const),
                  pl.BlockSpec((1, SSD_CONV_CH), const)],
        out_specs=[pl.BlockSpec((tm, SSD_D_INNER), row),
                   pl.BlockSpec((tm, SSD_CONV_CH), row),
                   pl.BlockSpec((tm, SSD_HEADS), row)],
        out_shape=[jax.ShapeDtypeStruct((s, SSD_D_INNER), F32),
                   jax.ShapeDtypeStruct((s, SSD_CONV_CH), F32),
                   jax.ShapeDtypeStruct((s, SSD_HEADS), F32)],
        scratch_shapes=[pltpu.VMEM((tm + SUBLANES, tc), F32),
                        pltpu.VMEM((SUBLANES, SSD_CONV_CH), F32)],
        compiler_params=_cparams(("arbitrary",)),
        name="odd_in",
    )(x, mod3, g, w_in_t, conv_w, conv_b)


SSD_TB = 256
SSD_CHUNK = 128


def _ssd_kernel(xbc_ref, dt_ref, z_ref, alog_ref, dtb_ref, dskip_ref, ng_ref, o_ref, hs_ref):
    tb, lc = SSD_TB, SSD_CHUNK

    @pl.when(pl.program_id(0) == 0)
    def _():
        hs_ref[...] = jnp.zeros_like(hs_ref)

    lane = lax.broadcasted_iota(jnp.int32, (tb, LANES), 1)
    dt_raw = jnp.concatenate([dt_ref[...], jnp.zeros((tb, LANES - SSD_HEADS), F32)], axis=1)
    dt = jnp.where(lane < SSD_HEADS, _softplus(dt_raw + dtb_ref[...]), 0.0)
    da = dt * (-jnp.exp(alog_ref[...]))
    ri = lax.broadcasted_iota(jnp.int32, (tb, tb), 0)
    ci = lax.broadcasted_iota(jnp.int32, (tb, tb), 1)
    acs = _mask_dot(((ri // lc) == (ci // lc)) & (ci <= ri), da)
    acs = acs * math.log2(math.e)
    acs_t = acs.T
    ldt_t = jnp.log2(jnp.where(lane < SSD_HEADS, dt, 1.0)).T
    left = lax.broadcasted_iota(jnp.int32, (lc, LANES), 1) < SSD_HEADDIM
    halves = (left, jnp.logical_not(left))
    causal = jnp.where(lax.broadcasted_iota(jnp.int32, (lc, lc), 1) <= lax.broadcasted_iota(jnp.int32, (lc, lc), 0),
                       0.0, -1e30)
    groups = range(SSD_GROUPS)
    pairs = range(SSD_HEADS_PER_GROUP // 2)

    def gcols(g):
        return slice(g * SSD_GROUP_W, (g + 1) * SSD_GROUP_W)

    def pcols(pr):
        return slice(pr * LANES, (pr + 1) * LANES)

    states = [[hs_ref[g, :, pcols(pr)] for pr in pairs] for g in groups]
    for sc in range(tb // lc):
        rs = slice(sc * lc, (sc + 1) * lc)
        acs_c, acs_tc, ldt_tc = acs[rs], acs_t[:, rs], ldt_t[:, rs]
        a_last = acs_c[lc - 1:lc, :]
        c_decay = jnp.exp2(a_last)
        xs = [xbc_ref[rs, gcols(g)] for g in groups]
        bg_t = [xbc_ref[rs, SSD_D_INNER + g * SSD_STATE:SSD_D_INNER + (g + 1) * SSD_STATE].T for g in groups]
        cg = [xbc_ref[rs, SSD_D_INNER + (SSD_GROUPS + g) * SSD_STATE:SSD_D_INNER + (SSD_GROUPS + g + 1) * SSD_STATE]
              for g in groups]
        cb = [lax.dot_general(cg[g].astype(BF16), bg_t[g].astype(BF16), (((1,), (0,)), ((), ())),
                              preferred_element_type=F32) for g in groups]
        ys = [[None] * len(pairs) for _ in groups]
        new_states = [[None] * len(pairs) for _ in groups]
        for pr in pairs:
            for g in groups:
                x_pair = xs[g][:, pcols(pr)]
                st = states[g][pr]
                acc = None
                upd = None
                for half in range(2):
                    hd = g * SSD_HEADS_PER_GROUP + 2 * pr + half
                    colb = jnp.broadcast_to(acs_c[:, hd:hd + 1], (lc, LANES))
                    lrow = ldt_tc[hd:hd + 1, :] - acs_tc[hd:hd + 1, :]
                    m1 = cb[g] * jnp.exp2(colb + (lrow + causal))
                    c1 = cg[g] * jnp.exp2(colb)
                    xh = jnp.where(halves[half], x_pair, 0.0).astype(BF16)
                    sh = jnp.where(halves[half], st, 0.0).astype(BF16)
                    t = jnp.dot(jnp.concatenate([m1, c1], axis=1).astype(BF16), jnp.concatenate([xh, sh], axis=0),
                                preferred_element_type=F32)
                    acc = t if acc is None else acc + t
                    wrow = jnp.exp2(a_last[:, hd:hd + 1] + lrow)
                    u = jnp.dot((bg_t[g] * wrow).astype(BF16), xh, preferred_element_type=F32)
                    upd = u if upd is None else upd + u
                ys[g][pr] = acc
                h0 = g * SSD_HEADS_PER_GROUP + 2 * pr
                cdec = jnp.where(left[0:1], c_decay[:, h0:h0 + 1], c_decay[:, h0 + 1:h0 + 2])
                new_states[g][pr] = st * cdec + upd
        states = new_states
        for g in groups:
            y = jnp.concatenate(ys[g], axis=1) + dskip_ref[:, gcols(g)] * xs[g]
            y = y * _silu(z_ref[rs, gcols(g)])
            ms = jnp.mean(y * y, axis=-1, keepdims=True)
            o_ref[rs, gcols(g)] = (y * lax.rsqrt(ms + NORM_EPS) * ng_ref[:, gcols(g)]).astype(BF16)
    for g in groups:
        for pr in pairs:
            hs_ref[g, :, pcols(pr)] = states[g][pr]


def _ssd(xbc, dt, z, alog_row, dtb_row, dskip_row, ng):
    s = xbc.shape[0]
    tb = SSD_TB
    const = lambda i: (0, 0)
    row = lambda i: (i, 0)
    return pl.pallas_call(
        _ssd_kernel,
        grid=(s // tb,),
        in_specs=[pl.BlockSpec((tb, SSD_CONV_CH), row),
                  pl.BlockSpec((tb, SSD_HEADS), row),
                  pl.BlockSpec((tb, SSD_D_INNER), row),
                  pl.BlockSpec((1, LANES), const),
                  pl.BlockSpec((1, LANES), const),
                  pl.BlockSpec((1, SSD_D_INNER), const),
                  pl.BlockSpec((1, SSD_D_INNER), const)],
        out_specs=pl.BlockSpec((tb, SSD_D_INNER), row),
        out_shape=jax.ShapeDtypeStruct((s, SSD_D_INNER), BF16),
        scratch_shapes=[pltpu.VMEM((SSD_GROUPS, SSD_STATE, SSD_GROUP_W), F32)],
        compiler_params=_cparams(("arbitrary",)),
        name="ssd",
    )(xbc, dt, z, alog_row, dtb_row, dskip_row, ng)


def _pad_lanes(v, lane0):
    return jnp.zeros((1, LANES), F32).at[0, lane0:lane0 + v.shape[0]].set(v.astype(F32))


def _swap_halves(w):
    half = w.shape[-1] // 2
    return jnp.concatenate([w[..., half:], w[..., :half]], axis=-1)


def _even_w_in(w):
    qkv, z, beta, a, cq, ckv, kr = jnp.split(
        w, [GDN_CONV_CH, GDN_CONV_CH + GDN_WIDTH, GDN_CONV_CH + GDN_WIDTH + GDN_HEADS,
            GDN_CONV_CH + GDN_WIDTH + 2 * GDN_HEADS, GDN_CONV_CH + GDN_WIDTH + 2 * GDN_HEADS + MLA_Q_RANK,
            GDN_CONV_CH + GDN_WIDTH + 2 * GDN_HEADS + MLA_Q_RANK + MLA_KV_RANK], axis=1)
    pad = jnp.zeros((w.shape[0], LANES - 2 * GDN_HEADS), w.dtype)
    return jnp.concatenate([qkv, z, cq, ckv, kr, _swap_halves(kr), beta, a, pad], axis=1).astype(BF16)


def _even_w_uq(w):
    pe = w[..., MLA_NOPE:]
    return jnp.concatenate([w, _swap_halves(pe)], axis=-1).reshape(MLA_Q_RANK, MLA_HEADS * 2 * LANES).astype(BF16)


def kernel(x, c, positions, ada_w, ada_b, norm_g, ffn_w1, ffn_w3, ffn_w2, ev_w_in, gdn_conv_w, gdn_A_log, gdn_dt_bias, gdn_norm_g, mla_q_norm_g, mla_w_uq, mla_kv_norm_g, mla_w_ukv, ev_w_out, ssd_w_in, ssd_conv_w, ssd_conv_b, ssd_A_log, ssd_dt_bias, ssd_D, ssd_norm_g, ssd_w_out, final_g):
    b, s, d = x.shape
    assert b == 1 and d == D_MODEL and s % (2 * FLASH_T) == 0
    xs = x.reshape(s, d)
    mod = _modulation(c.reshape(d, 1), ada_w, ada_b).reshape(DEPTH, 3, 3, d)
    rope = _rope_table(positions.reshape(s, 1))
    fg = final_g.reshape(1, d)
    wb = (ffn_w1[0, 0].astype(BF16), ffn_w3[0, 0].astype(BF16), ffn_w2[0, 0].astype(BF16))
    for l in range(DEPTH):
        last = l == DEPTH - 1
        if l % 2:
            side = [(jnp.swapaxes(ssd_w_in, 1, 2), l // 2, D_MODEL, OD_IN_CAST_STEPS),
                    (ssd_w_out, l // 2, D_MODEL, None)]
        else:
            side = [(ev_w_in, l // 2, EV_IN_P, None), (ev_w_out, l // 2, D_MODEL, None)]
        xs, wb, side = _ffn(xs, mod[l, 0], norm_g[l, 0].reshape(1, d), wb, fg, False,
                            cast_next=(ffn_w1, ffn_w3, ffn_w2, l, 1), cast_side=side)
        g1 = norm_g[l, 1].reshape(1, d)
        if l % 2 == 0:
            e = l // 2
            w_in_b, w_out_b = side
            qkv, z, ba, q, k, v = _even_in(
                xs, mod[l, 1], g1, w_in_b, gdn_conv_w[e], rope,
                mla_q_norm_g[e].reshape(1, -1), _even_w_uq(mla_w_uq[e]),
                mla_kv_norm_g[e].reshape(1, -1),
                mla_w_ukv[e].reshape(MLA_KV_RANK, MLA_HEADS * 2 * LANES).astype(BF16))
            o_a = _gdn(qkv, ba, z, _pad_lanes(gdn_A_log[e], GDN_HEADS), _pad_lanes(gdn_dt_bias[e], GDN_HEADS),
                       gdn_norm_g[e].reshape(1, -1))
            o_b = _flash(q, k, v)
            mix = (mod[l, 1], [o_a, o_b], w_out_b)
        else:
            o = l // 2
            w_in_b, w_out_b = side
            z, xbc, dt = _odd_in(xs, mod[l, 1], g1, w_in_b, ssd_conv_w[o], ssd_conv_b[o].reshape(1, -1))
            y = _ssd(xbc, dt, z, _pad_lanes(ssd_A_log[o], 0), _pad_lanes(ssd_dt_bias[o], 0),
                     jnp.repeat(ssd_D[o].astype(F32), SSD_HEADDIM).reshape(1, -1), ssd_norm_g[o].reshape(1, -1))
            mix = (mod[l, 1], [y], w_out_b)
        if last:
            xs = _ffn(xs, mod[l, 2], norm_g[l, 2].reshape(1, d), wb, fg, True, mix=mix)
        else:
            xs, wb, _ = _ffn(xs, mod[l, 2], norm_g[l, 2].reshape(1, d), wb, fg, False, mix=mix,
                             cast_next=(ffn_w1, ffn_w3, ffn_w2, l + 1, 0))
    return xs.reshape(b, s, d)
```
